```python
import math
import jax, jax.numpy as jnp
from jax import lax
import numpy as np

D_MODEL = 2048
BATCH = 4
SEQ = 2048
DEPTH = 4
DEC_BATCH = 8
DEC_SEQ = 4
PAST_LEN = 16384
PAGE_SIZE = 128

N_MIXERS = 2
N_SSD_LAYERS = (DEPTH + N_MIXERS - 1) // N_MIXERS
N_ATTN_LAYERS = DEPTH // N_MIXERS
D_FF = 5632
PLE_DIM = 256
SSD_EXPAND = 2
D_INNER = SSD_EXPAND * D_MODEL
SSD_HEAD_DIM = 64
SSD_HEADS = D_INNER // SSD_HEAD_DIM
SSD_GROUPS = 8
SSD_HPG = SSD_HEADS // SSD_GROUPS
D_STATE = 128
CONV_W = 4
CONV_DIM = D_INNER + 2 * SSD_GROUPS * D_STATE
SSD_IN_DIM = D_INNER + CONV_DIM + SSD_HEADS
SSD_CHUNK = 128
ATT_HEAD_DIM = 64
ATT_HEADS = D_MODEL // (2 * ATT_HEAD_DIM)
ATT_WIDTH = ATT_HEADS * 2 * ATT_HEAD_DIM
Q_BLOCK = 128
NORM_EPS = 1e-6

kernel_name = "hybrid_ssd_diffattn_macaron_step"


def rms_norm(x, g, eps=NORM_EPS):
    xf = x.astype(jnp.float32)
    y = xf * lax.rsqrt(jnp.mean(xf * xf, axis=-1, keepdims=True) + eps)
    return (y * g.astype(jnp.float32)).astype(x.dtype)


def swiglu_half(h, g, w_in, w_out):
    u = rms_norm(h, g) @ w_in
    a, b = jnp.split(u, 2, axis=-1)
    return h + 0.5 * ((jax.nn.silu(a) * b) @ w_out)


def ple_add(h, p_i, g, w_proj, w_gate):
    gate = jax.nn.sigmoid((rms_norm(h, g) @ w_gate).astype(jnp.float32)).astype(h.dtype)
    return h + (p_i @ w_proj).astype(h.dtype) * gate


def ssd_chunk_scan(X, A, B, C, h0):
    bsz, L = X.shape[:2]
    q = math.gcd(L, SSD_CHUNK)
    c = L // q
    X = X.reshape(bsz, c, q, *X.shape[2:])
    A = A.reshape(bsz, c, q, *A.shape[2:])
    B = B.reshape(bsz, c, q, *B.shape[2:])
    C = C.reshape(bsz, c, q, *C.shape[2:])
    A_cs = jnp.cumsum(A, axis=2)
    seg = A_cs[:, :, :, None] - A_cs[:, :, None, :]
    causal = jnp.tril(jnp.ones((q, q), dtype=bool))[None, None, :, :, None, None]
    decay_ls = jnp.exp(jnp.where(causal, seg, -jnp.inf))
    cb = jnp.einsum('bclgn,bcsgn->bclsg', C, B)
    y_diag = jnp.einsum('bclsg,bclsgr,bcsgrp->bclgrp', cb, decay_ls, X)
    decay_to_end = jnp.exp(A_cs[:, :, -1:] - A_cs)
    chunk_states = jnp.einsum('bcsgn,bcsgr,bcsgrp->bcgrpn', B, decay_to_end, X)
    chunk_decay = jnp.exp(A_cs[:, :, -1])

    def step(h, inp):
        dec, st = inp
        return dec[..., None, None] * h + st, h

    h_last, h_in = lax.scan(step, h0, (jnp.moveaxis(chunk_decay, 1, 0), jnp.moveaxis(chunk_states, 1, 0)))
    h_in = jnp.moveaxis(h_in, 0, 1)
    y_off = jnp.einsum('bclgn,bcgrpn,bclgr->bclgrp', C, h_in, jnp.exp(A_cs))
    return (y_diag + y_off).reshape(bsz, L, *X.shape[3:]), h_last


def ssd_mixer(u, conv_buf, h0, w_in, conv_w, conv_b, dt_bias, a_log, d_skip, norm_g, w_out):
    f32 = jnp.float32
    bsz, L, _ = u.shape
    zxbcdt = u @ w_in
    z = zxbcdt[..., :D_INNER]
    xbc = zxbcdt[..., D_INNER:D_INNER + CONV_DIM]
    dt_raw = zxbcdt[..., D_INNER + CONV_DIM:]
    xpad = jnp.concatenate([conv_buf.astype(xbc.dtype), xbc], axis=1)
    new_conv = xpad[:, L:]
    conv = xpad[:, 0:L] * conv_w[0]
    for k in range(1, CONV_W):
        conv = conv + xpad[:, k:k + L] * conv_w[k]
    xbc = jax.nn.silu(conv + conv_b)
    gn = SSD_GROUPS * D_STATE
    x = xbc[..., :D_INNER].reshape(bsz, L, SSD_GROUPS, SSD_HPG, SSD_HEAD_DIM).astype(f32)
    Bm = xbc[..., D_INNER:D_INNER + gn].reshape(bsz, L, SSD_GROUPS, D_STATE).astype(f32)
    Cm = xbc[..., D_INNER + gn:].reshape(bsz, L, SSD_GROUPS, D_STATE).astype(f32)
    dt = jax.nn.softplus(dt_raw.astype(f32) + dt_bias.astype(f32)).reshape(bsz, L, SSD_GROUPS, SSD_HPG)
    A = -jnp.exp(a_log.astype(f32)).reshape(SSD_GROUPS, SSD_HPG)
    h0f = h0.astype(f32).reshape(bsz, SSD_GROUPS, SSD_HPG, SSD_HEAD_DIM, D_STATE)
    y, h_last = ssd_chunk_scan(x * dt[..., None], dt * A, Bm, Cm, h0f)
    y = y + x * d_skip.astype(f32).reshape(SSD_GROUPS, SSD_HPG)[:, :, None]
    y = y.reshape(bsz, L, D_INNER) * jax.nn.silu(z.astype(f32))
    yg = y.reshape(bsz, L, SSD_GROUPS, D_INNER // SSD_GROUPS)
    yg = yg * lax.rsqrt(jnp.mean(yg * yg, axis=-1, keepdims=True) + NORM_EPS)
    y = (yg.reshape(bsz, L, D_INNER) * norm_g.astype(f32)).astype(u.dtype)
    h_new = h_last.reshape(bsz, SSD_HEADS, SSD_HEAD_DIM, D_STATE).astype(h0.dtype)
    return y @ w_out, new_conv, h_new


def diff_attn(u, k_past, v_past, layer_idx, w_qkv, qk_norm, lam_p, sub_norm, w_out):
    f32 = jnp.float32
    bsz, L, _ = u.shape
    H, HD = ATT_HEADS, ATT_HEAD_DIM
    qkv = u @ w_qkv
    q = rms_norm(qkv[..., :ATT_WIDTH].reshape(bsz, L, H, 2, HD), qk_norm[0])
    k = rms_norm(qkv[..., ATT_WIDTH:2 * ATT_WIDTH].reshape(bsz, L, H, 2, HD), qk_norm[1])
    v = qkv[..., 2 * ATT_WIDTH:].reshape(bsz, L, H, 2 * HD)
    k_rows = k.reshape(bsz, L, H, 2 * HD)
    if k_past is None:
        k_all, v_all = k_rows, v
    else:
        k_all = jnp.concatenate([k_past.astype(k_rows.dtype), k_rows], axis=1)
        v_all = jnp.concatenate([v_past.astype(v.dtype), v], axis=1)
    Lk = k_all.shape[1]
    q_off = Lk - L
    lambda_init = 0.8 - 0.6 * math.exp(-0.3 * layer_idx)
    lf = lam_p.astype(f32)
    lam = jnp.exp(jnp.sum(lf[0] * lf[1])) - jnp.exp(jnp.sum(lf[2] * lf[3])) + lambda_init
    slopes = jnp.exp2(-8.0 * jnp.arange(1, H + 1, dtype=f32) / H)
    scale = 1.0 / math.sqrt(HD)
    k_heads = k_all.reshape(bsz, Lk, H, 2, HD)
    k_pos = jnp.arange(Lk, dtype=jnp.int32)
    qb = math.gcd(L, Q_BLOCK)
    nb = L // qb
    q_blocks = jnp.moveaxis(q.reshape(bsz, nb, qb, H, 2, HD), 1, 0)
    starts = q_off + qb * jnp.arange(nb, dtype=jnp.int32)

    def block(args):
        qblk, start = args
        q_pos = start + jnp.arange(qb, dtype=jnp.int32)
        dist = q_pos[:, None] - k_pos[None, :]
        s = jnp.einsum('bqhmd,bkhmd->bhmqk', qblk, k_heads, preferred_element_type=f32) * scale
        s = s - slopes[None, :, None, None, None] * dist.astype(f32)
        s = jnp.where(dist >= 0, s, -jnp.inf)
        pr = jax.nn.softmax(s, axis=-1)
        w = pr[:, :, 0] - lam * pr[:, :, 1]
        return jnp.einsum('bhqk,bkhe->bqhe', w.astype(v_all.dtype), v_all)

    o = lax.map(block, (q_blocks, starts))
    o = jnp.moveaxis(o, 0, 1).reshape(bsz, L, H, 2 * HD)
    o = rms_norm(o, sub_norm) * (1.0 - lambda_init)
    return o.reshape(bsz, L, ATT_WIDTH) @ w_out, k_rows, v


def setup_inputs(seed: int = 0) -> dict:
    key = jax.random.key(seed)
    ks = iter(jax.random.split(key, 48))
    f32 = jnp.float32

    def nrm(shape, scale):
        return scale * jax.random.normal(next(ks), shape, f32)

    def gain(shape):
        return 1.0 + 0.02 * jax.random.normal(next(ks), shape, f32)

    n_pages = PAST_LEN // PAGE_SIZE
    used = DEC_BATCH * n_pages
    n_phys = used + max(1, used // 4)
    page_table = jax.random.permutation(next(ks), n_phys)[:used].reshape(DEC_BATCH, n_pages).astype(jnp.int32)
    dt0 = jnp.exp(jax.random.uniform(next(ks), (N_SSD_LAYERS, SSD_HEADS), f32, math.log(1e-3), math.log(1e-1)))
    dt_bias = dt0 + jnp.log(-jnp.expm1(-dt0))
    a_log = jnp.log(jax.random.uniform(next(ks), (N_SSD_LAYERS, SSD_HEADS), f32, 1.0, 16.0))
    kv_shape = (N_ATTN_LAYERS, n_phys, PAGE_SIZE, ATT_HEADS, 2 * ATT_HEAD_DIM)
    return {
        'x_prompt': nrm((BATCH, SEQ, D_MODEL), 1.0),
        'x_sample': nrm((DEC_BATCH, DEC_SEQ, D_MODEL), 1.0),
        'cache_k': nrm(kv_shape, 1.0),
        'cache_v': nrm(kv_shape, 1.0),
        'state_conv': nrm((N_SSD_LAYERS, DEC_BATCH, CONV_W - 1, CONV_DIM), 1.0),
        'state_ssm': nrm((N_SSD_LAYERS, DEC_BATCH, SSD_HEADS, SSD_HEAD_DIM, D_STATE), 0.5),
        'page_table': page_table,
        'p_prompt': nrm((DEPTH, BATCH, SEQ, PLE_DIM), 1.0),
        'p_sample': nrm((DEPTH, DEC_BATCH, DEC_SEQ, PLE_DIM), 1.0),
        'norm_ffa': gain((DEPTH, D_MODEL)),
        'ffa_in': nrm((DEPTH, D_MODEL, 2 * D_FF), D_MODEL ** -0.5),
        'ffa_out': nrm((DEPTH, D_FF, D_MODEL), D_FF ** -0.5),
        'norm_mix': gain((DEPTH, D_MODEL)),
        'norm_ffb': gain((DEPTH, D_MODEL)),
        'ffb_in': nrm((DEPTH, D_MODEL, 2 * D_FF), D_MODEL ** -0.5),
        'ffb_out': nrm((DEPTH, D_FF, D_MODEL), D_FF ** -0.5),
        'norm_ple': gain((DEPTH, D_MODEL)),
        'ple_proj': nrm((DEPTH, PLE_DIM, D_MODEL), PLE_DIM ** -0.5),
        'ple_gate': nrm((DEPTH, D_MODEL, D_MODEL), D_MODEL ** -0.5),
        'ssd_in': nrm((N_SSD_LAYERS, D_MODEL, SSD_IN_DIM), D_MODEL ** -0.5),
        'ssd_conv_w': nrm((N_SSD_LAYERS, CONV_W, CONV_DIM), CONV_W ** -0.5),
        'ssd_conv_b': nrm((N_SSD_LAYERS, CONV_DIM), 0.02),
        'ssd_dt_bias': dt_bias,
        'ssd_a_log': a_log,
        'ssd_d': gain((N_SSD_LAYERS, SSD_HEADS)),
        'ssd_norm': gain((N_SSD_LAYERS, D_INNER)),
        'ssd_out': nrm((N_SSD_LAYERS, D_INNER, D_MODEL), D_INNER ** -0.5),
        'attn_qkv': nrm((N_ATTN_LAYERS, D_MODEL, 3 * ATT_WIDTH), D_MODEL ** -0.5),
        'attn_qk_norm': gain((N_ATTN_LAYERS, 2, 2, ATT_HEAD_DIM)),
        'attn_lambda': nrm((N_ATTN_LAYERS, 4, ATT_HEAD_DIM), 0.1),
        'attn_sub_norm': gain((N_ATTN_LAYERS, 2 * ATT_HEAD_DIM)),
        'attn_out': nrm((N_ATTN_LAYERS, ATT_WIDTH, D_MODEL), ATT_WIDTH ** -0.5),
    }


def reference(x_prompt, x_sample, cache_k, cache_v, state_conv, state_ssm, page_table, p_prompt, p_sample,
              norm_ffa, ffa_in, ffa_out, norm_mix, norm_ffb, ffb_in, ffb_out, norm_ple, ple_proj, ple_gate,
              ssd_in, ssd_conv_w, ssd_conv_b, ssd_dt_bias, ssd_a_log, ssd_d, ssd_norm, ssd_out,
              attn_qkv, attn_qk_norm, attn_lambda, attn_sub_norm, attn_out):
    hp, hs = x_prompt, x_sample
    bp, bs = x_prompt.shape[0], x_sample.shape[0]
    kp_l, vp_l, ks_l, vs_l = [], [], [], []
    cp_l, sp_l, cs_l, ss_l = [], [], [], []
    for i in range(DEPTH):
        j = i // N_MIXERS
        hp = swiglu_half(hp, norm_ffa[i], ffa_in[i], ffa_out[i])
        hs = swiglu_half(hs, norm_ffa[i], ffa_in[i], ffa_out[i])
        up = rms_norm(hp, norm_mix[i])
        us = rms_norm(hs, norm_mix[i])
        if i % N_MIXERS == 0:
            w = (ssd_in[j], ssd_conv_w[j], ssd_conv_b[j], ssd_dt_bias[j], ssd_a_log[j], ssd_d[j], ssd_norm[j], ssd_out[j])
            conv0 = jnp.zeros((bp, CONV_W - 1, CONV_DIM), state_conv.dtype)
            h0 = jnp.zeros((bp, SSD_HEADS, SSD_HEAD_DIM, D_STATE), state_ssm.dtype)
            mp, cp, sp = ssd_mixer(up, conv0, h0, *w)
            ms, cs, ss = ssd_mixer(us, state_conv[j], state_ssm[j], *w)
            cp_l.append(cp); sp_l.append(sp); cs_l.append(cs); ss_l.append(ss)
        else:
            w = (attn_qkv[j], attn_qk_norm[j], attn_lambda[j], attn_sub_norm[j], attn_out[j])
            mp, kp, vp = diff_attn(up, None, None, i, *w)
            k_past = cache_k[j, page_table].reshape(bs, -1, ATT_HEADS, 2 * ATT_HEAD_DIM)
            v_past = cache_v[j, page_table].reshape(bs, -1, ATT_HEADS, 2 * ATT_HEAD_DIM)
            ms, ks, vs = diff_attn(us, k_past, v_past, i, *w)
            kp_l.append(kp); vp_l.append(vp); ks_l.append(ks); vs_l.append(vs)
        hp = hp + mp
        hs = hs + ms
        hp = swiglu_half(hp, norm_ffb[i], ffb_in[i], ffb_out[i])
        hs = swiglu_half(hs, norm_ffb[i], ffb_in[i], ffb_out[i])
        hp = ple_add(hp, p_prompt[i], norm_ple[i], ple_proj[i], ple_gate[i])
        hs = ple_add(hs, p_sample[i], norm_ple[i], ple_proj[i], ple_gate[i])
    k_prompt = jnp.stack(kp_l)
    v_prompt = jnp.stack(vp_l)
    k_sample = jnp.stack(ks_l)
    v_sample = jnp.stack(vs_l)
    conv_prompt = jnp.stack(cp_l)
    ssm_prompt = jnp.stack(sp_l)
    conv_sample = jnp.stack(cs_l)
    ssm_sample = jnp.stack(ss_l)
    return (hp, hs, k_prompt, v_prompt, k_sample, v_sample, conv_prompt, ssm_prompt, conv_sample, ssm_sample)
```

```python
import functools
import math

import jax
import jax.numpy as jnp
from jax import lax
from jax.experimental import pallas as pl
from jax.experimental.pallas import tpu as pltpu

F32 = jnp.float32
BF16 = jnp.bfloat16
NORM_EPS = 1e-6
SSD_CHUNK = 128
LANES = 128
CONV_TAIL = 8
VMEM_LIMIT_BYTES = 56 * 1024 * 1024


def _cparams(*sem):
    return pltpu.CompilerParams(dimension_semantics=sem, vmem_limit_bytes=VMEM_LIMIT_BYTES)


def _pick(n, prefs):
    for t in prefs:
        if n % t == 0:
            return t
    return n


def _silu(x):
    return x * jax.nn.sigmoid(x)


def _rms(x, g):
    return x * lax.rsqrt(jnp.mean(x * x, axis=-1, keepdims=True) + NORM_EPS) * g


def _dot(a, b):
    return jnp.dot(a, b, preferred_element_type=F32)


def _dot_nt(a, b):
    return lax.dot_general(a, b, (((1,), (1,)), ((), ())), preferred_element_type=F32)


def _split3(x):
    hi = x.astype(BF16)
    r = x - hi.astype(F32)
    mid = r.astype(BF16)
    lo = (r - mid.astype(F32)).astype(BF16)
    return hi, mid, lo


def _ffn_kernel(h_ref, g_ref, wa_ref, wb_ref, wo_ref, o_ref, xn_ref, acc_ref):
    f = pl.program_id(1)

    @pl.when(f == 0)
    def _():
        xn_ref[...] = _rms(h_ref[...], g_ref[...]).astype(BF16)
        acc_ref[...] = jnp.zeros_like(acc_ref)

    xn = xn_ref[...]
    a = _dot(xn, wa_ref[...])
    b = _dot(xn, wb_ref[...])
    acc_ref[...] += _dot((_silu(a) * b).astype(BF16), wo_ref[...])

    @pl.when(f == pl.num_programs(1) - 1)
    def _():
        o_ref[...] = h_ref[...] + 0.5 * acc_ref[...]


def _ffn(h, g, w_in, w_out):
    m, d = h.shape
    dff = w_out.shape[0]
    tm = _pick(m, (512, 256, 128, 64, 32, 16, 8))
    tf = _pick(dff, (512, 256, 128))
    nf = dff // tf
    return pl.pallas_call(
        _ffn_kernel,
        grid=(m // tm, nf),
        in_specs=[
            pl.BlockSpec((tm, d), lambda i, f: (i, 0)),
            pl.BlockSpec((1, d), lambda i, f: (0, 0)),
            pl.BlockSpec((d, tf), lambda i, f: (0, f)),
            pl.BlockSpec((d, tf), lambda i, f: (0, f + nf)),
            pl.BlockSpec((tf, d), lambda i, f: (f, 0)),
        ],
        out_specs=pl.BlockSpec((tm, d), lambda i, f: (i, 0)),
        out_shape=jax.ShapeDtypeStruct((m, d), F32),
        scratch_shapes=[pltpu.VMEM((tm, d), BF16), pltpu.VMEM((tm, d), F32)],
        compiler_params=_cparams("parallel", "arbitrary"),
        name="ffn",
    )(h, g.reshape(1, d), w_in, w_in, w_out)


def _ple_kernel(h_ref, p_ref, g_ref, wp_ref, wg_ref, o_ref):
    h = h_ref[...]
    xn = _rms(h, g_ref[...]).astype(BF16)
    gate = jax.nn.sigmoid(_dot(xn, wg_ref[...]))
    proj = _dot(p_ref[...].astype(BF16), wp_ref[...])
    o_ref[...] = h + proj * gate


def _ple(h, p, g, w_proj, w_gate):
    m, d = h.shape
    pd = p.shape[1]
    tm = _pick(m, (256, 128, 64, 32, 16, 8))
    return pl.pallas_call(
        _ple_kernel,
        grid=(m // tm,),
        in_specs=[
            pl.BlockSpec((tm, d), lambda i: (i, 0)),
            pl.BlockSpec((tm, pd), lambda i: (i, 0)),
            pl.BlockSpec((1, d), lambda i: (0, 0)),
            pl.BlockSpec((pd, d), lambda i: (0, 0)),
            pl.BlockSpec((d, d), lambda i: (0, 0)),
        ],
        out_specs=pl.BlockSpec((tm, d), lambda i: (i, 0)),
        out_shape=jax.ShapeDtypeStruct((m, d), F32),
        compiler_params=_cparams("parallel"),
        name="ple",
    )(h, p, g.reshape(1, d), w_proj, w_gate)


def _seg_meansq(y, seg_ref, hd):
    y2 = y * y
    hi = y2.astype(BF16)
    lo = (y2 - hi.astype(F32)).astype(BF16)
    seg = seg_ref[...]
    return (_dot(hi, seg) + _dot(lo, seg)) * (1.0 / hd)


def _nmm_kernel(*refs, mode, hd):
    if mode in ("q", "k"):
        h_ref, g_ref, w_ref, gn_ref, seg_ref = refs[:5]
        outs = refs[5:-1]
    else:
        h_ref, g_ref, w_ref = refs[:3]
        outs = refs[3:-1]
    xn_ref = refs[-1]

    @pl.when(pl.program_id(1) == 0)
    def _():
        xn_ref[...] = _rms(h_ref[...], g_ref[...]).astype(BF16)

    y = _dot(xn_ref[...], w_ref[...])
    if mode == "plain":
        outs[0][...] = y
    elif mode == "v":
        outs[0][...] = y
        outs[1][...] = y.astype(BF16)
    else:
        yn = y * lax.rsqrt(_seg_meansq(y, seg_ref, hd) + NORM_EPS) * gn_ref[...]
        if mode == "k":
            outs[0][...] = yn
            outs[1][...] = yn.astype(BF16)
        else:
            yn = yn * (1.0 / math.sqrt(hd))
            lane = lax.broadcasted_iota(jnp.int32, yn.shape, 1)
            first = (lane % (2 * hd)) < hd
            outs[0][...] = jnp.where(first, yn, 0.0).astype(BF16)
            outs[1][...] = jnp.where(first, 0.0, yn).astype(BF16)


def _norm_matmul(h, g, w, mode="plain", gain=None, hd=None, tn_prefs=(1024, 512, 256, 128)):
    m, d = h.shape
    n = w.shape[1]
    tm = _pick(m, (512, 256, 128, 64, 32, 16, 8))
    tn = _pick(n, (256, 128) if mode in ("q", "k") else tn_prefs)
    in_specs = [
        pl.BlockSpec((tm, d), lambda i, j: (i, 0)),
        pl.BlockSpec((1, d), lambda i, j: (0, 0)),
        pl.BlockSpec((d, tn), lambda i, j: (0, j)),
    ]
    args = [h, g.reshape(1, d), w]
    if mode in ("q", "k"):
        gain_row = jnp.tile(gain.reshape(1, 2 * hd).astype(F32), (1, tn // (2 * hd)))
        lane = jnp.arange(tn)
        seg = (lane[:, None] // hd == lane[None, :] // hd).astype(BF16)
        in_specs += [pl.BlockSpec((1, tn), lambda i, j: (0, 0)),
                     pl.BlockSpec((tn, tn), lambda i, j: (0, 0))]
        args += [gain_row, seg]
    out_dtypes = {"plain": (F32,), "q": (BF16, BF16), "k": (F32, BF16), "v": (F32, BF16)}[mode]
    out = pl.pallas_call(
        functools.partial(_nmm_kernel, mode=mode, hd=hd),
        grid=(m // tm, n // tn),
        in_specs=in_specs,
        out_specs=[pl.BlockSpec((tm, tn), lambda i, j: (i, j)) for _ in out_dtypes],
        out_shape=[jax.ShapeDtypeStruct((m, n), dt) for dt in out_dtypes],
        scratch_shapes=[pltpu.VMEM((tm, d), BF16)],
        compiler_params=_cparams("parallel", "arbitrary"),
        name="norm_matmul_" + mode,
    )(*args)
    return out[0] if mode == "plain" else out


def _mm_res_kernel(h_ref, x_ref, w_ref, o_ref):
    o_ref[...] = h_ref[...] + _dot(x_ref[...].astype(BF16), w_ref[...])


def _matmul_residual(h, x, w):
    m, n = h.shape
    k = x.shape[1]
    tm = _pick(m, (512, 256, 128, 64, 32, 16, 8))
    tn = _pick(n, (512, 256, 128))
    return pl.pallas_call(
        _mm_res_kernel,
        grid=(m // tm, n // tn),
        in_specs=[
            pl.BlockSpec((tm, tn), lambda i, j: (i, j)),
            pl.BlockSpec((tm, k), lambda i, j: (i, 0)),
            pl.BlockSpec((k, tn), lambda i, j: (0, j)),
        ],
        out_specs=pl.BlockSpec((tm, tn), lambda i, j: (i, j)),
        out_shape=jax.ShapeDtypeStruct((m, n), F32),
        compiler_params=_cparams("parallel", "arbitrary"),
        name="matmul_residual",
    )(h, x, w)


def _ssd_kernel(*refs, lc, valid, has_init, hpg, pdim):
    (z_ref, x_ref, b_ref, c_ref, dt_ref) = refs[:5]
    pos = 5
    if has_init:
        tx_ref, tb_ref, tc_ref, h0_ref = refs[pos:pos + 4]
        pos += 4
    (cwx_ref, cwb_ref, cwc_ref, cbx_ref, cbb_ref, cbc_ref,
     dtb_ref, alog_ref, dsk_ref, ng_ref) = refs[pos:pos + 10]
    pos += 10
    y_ref, hout_ref = refs[pos:pos + 2]
    ht_ref, ex_ref, eb_ref, ec_ref = refs[pos + 2:]

    c = pl.program_id(2)
    w = hpg * pdim
    nst = ht_ref.shape[0]
    t0 = CONV_TAIL

    @pl.when(c == 0)
    def _():
        if has_init:
            ex_ref[0:t0, :] = tx_ref[0]
            eb_ref[0:t0, :] = tb_ref[0]
            ec_ref[0:t0, :] = tc_ref[0]
            ht_ref[...] = h0_ref[0].reshape(w, nst).T
        else:
            ex_ref[0:t0, :] = jnp.zeros((t0, w), F32)
            eb_ref[0:t0, :] = jnp.zeros((t0, nst), F32)
            ec_ref[0:t0, :] = jnp.zeros((t0, nst), F32)
            ht_ref[...] = jnp.zeros_like(ht_ref)

    def conv_act(e_ref, raw_ref, cw_ref, cb_ref):
        e_ref[t0:t0 + lc, :] = raw_ref[...]
        kw = cw_ref.shape[0]
        acc = cb_ref[...] + cw_ref[kw - 1:kw, :] * e_ref[t0:t0 + lc, :]
        for k in range(kw - 1):
            off = t0 - (kw - 1) + k
            acc = acc + cw_ref[k:k + 1, :] * e_ref[off:off + lc, :]
        e_ref[0:t0, :] = e_ref[lc:lc + t0, :]
        return _silu(acc)

    xc = conv_act(ex_ref, x_ref, cwx_ref, cbx_ref)
    bc = conv_act(eb_ref, b_ref, cwb_ref, cbb_ref)
    cc = conv_act(ec_ref, c_ref, cwc_ref, cbc_ref)

    row = lax.broadcasted_iota(jnp.int32, (lc, lc), 0)
    col = lax.broadcasted_iota(jnp.int32, (lc, lc), 1)
    causal = row >= col
    lane = lax.broadcasted_iota(jnp.int32, (lc, LANES), 1)

    dtr = dt_ref[...] + dtb_ref[...]
    dt = jnp.maximum(dtr, 0.0) + jnp.log1p(jnp.exp(-jnp.abs(dtr)))
    if valid < lc:
        dt = jnp.where(lax.broadcasted_iota(jnp.int32, (lc, LANES), 0) < valid, dt, 0.0)
    da = dt * (-jnp.exp(alog_ref[...]))
    tri = causal.astype(BF16)
    d_hi, d_mid, d_lo = _split3(da)
    cs = _dot(tri, d_hi) + _dot(tri, d_mid) + _dot(tri, d_lo)
    cs_t = cs.T
    ecs = jnp.exp(cs)
    coef_s = dt * jnp.exp(cs[lc - 1:lc, :] - cs)

    def head_col(arr, j):
        return jnp.sum(jnp.where(lane == j, arr, 0.0), axis=-1, keepdims=True)

    first_half = lane < pdim

    def expand(arr):
        blks = [jnp.where(first_half, head_col(arr, 2 * q), head_col(arr, 2 * q + 1))
                for q in range(hpg // 2)]
        return blks[0] if len(blks) == 1 else jnp.concatenate(blks, axis=1)

    dt_x = expand(dt)
    coef_x = expand(coef_s)
    ecs_x = expand(ecs)

    bcb = bc.astype(BF16)
    ccb = cc.astype(BF16)
    cb = _dot_nt(ccb, bcb)
    xd = xc * dt_x

    def decay_mat(j):
        seg = head_col(cs, j) - cs_t[j:j + 1, :]
        return (cb * jnp.exp(jnp.where(causal, seg, -jnp.inf))).astype(BF16)

    y_blks = []
    for q in range(hpg // 2):
        mcat = jnp.concatenate([decay_mat(2 * q), decay_mat(2 * q + 1)], axis=1)
        xp = xd[:, q * LANES:(q + 1) * LANES]
        rhs = jnp.concatenate([jnp.where(first_half, xp, 0.0), jnp.where(first_half, 0.0, xp)],
                              axis=0).astype(BF16)
        y_blks.append(_dot(mcat, rhs))
    y = y_blks[0] if len(y_blks) == 1 else jnp.concatenate(y_blks, axis=1)

    ht_in = ht_ref[...]
    y = y + _dot(ccb, ht_in.astype(BF16)) * ecs_x
    upd = _dot(bc.T.astype(BF16), (xc * coef_x).astype(BF16))
    ht_ref[...] = ht_in * ecs_x[lc - 1:lc, :] + upd

    y = y + xc * dsk_ref[...]
    y = y * _silu(z_ref[...])
    y = y * lax.rsqrt(jnp.mean(y * y, axis=-1, keepdims=True) + NORM_EPS) * ng_ref[...]
    y_ref[...] = y.astype(BF16)

    @pl.when(c == pl.num_programs(2) - 1)
    def _():
        hout_ref[0] = ht_ref[...].T.reshape(hpg, pdim, nst)


def _ssd_core(zx, bsz, n_chunks, valid, dims, prm, init=None):
    d_inner, nst, groups, heads = dims
    hpg = heads // groups
    pdim = d_inner // heads
    w = hpg * pdim
    lc = SSD_CHUNK
    assert pdim * 2 == LANES and hpg % 2 == 0 and nst == LANES
    rows = bsz * n_chunks * lc
    xo, bo, co, do = d_inner // w, 2 * d_inner // nst, (2 * d_inner + groups * nst) // nst, \
        (2 * d_inner + 2 * groups * nst) // LANES
    cxo, cbo, cco = 0, d_inner // nst, (d_inner + groups * nst) // nst

    def rowblk(b, g, c):
        return b * n_chunks + c

    in_specs = [
        pl.BlockSpec((lc, w), lambda b, g, c: (rowblk(b, g, c), g)),
        pl.BlockSpec((lc, w), lambda b, g, c: (rowblk(b, g, c), xo + g)),
        pl.BlockSpec((lc, nst), lambda b, g, c: (rowblk(b, g, c), bo + g)),
        pl.BlockSpec((lc, nst), lambda b, g, c: (rowblk(b, g, c), co + g)),
        pl.BlockSpec((lc, LANES), lambda b, g, c: (rowblk(b, g, c), do + g)),
    ]
    args = [zx, zx, zx, zx, zx]
    if init is not None:
        tail, h0 = init
        in_specs += [
            pl.BlockSpec((1, CONV_TAIL, w), lambda b, g, c: (b, 0, cxo + g)),
            pl.BlockSpec((1, CONV_TAIL, nst), lambda b, g, c: (b, 0, cbo + g)),
            pl.BlockSpec((1, CONV_TAIL, nst), lambda b, g, c: (b, 0, cco + g)),
            pl.BlockSpec((1, hpg, pdim, nst), lambda b, g, c: (b, g, 0, 0)),
        ]
        args += [tail, tail, tail, h0]
    kw = prm["conv_w"].shape[0]
    in_specs += [
        pl.BlockSpec((kw, w), lambda b, g, c: (0, cxo + g)),
        pl.BlockSpec((kw, nst), lambda b, g, c: (0, cbo + g)),
        pl.BlockSpec((kw, nst), lambda b, g, c: (0, cco + g)),
        pl.BlockSpec((1, w), lambda b, g, c: (0, cxo + g)),
        pl.BlockSpec((1, nst), lambda b, g, c: (0, cbo + g)),
        pl.BlockSpec((1, nst), lambda b, g, c: (0, cco + g)),
        pl.BlockSpec((1, LANES), lambda b, g, c: (0, g)),
        pl.BlockSpec((1, LANES), lambda b, g, c: (0, g)),
        pl.BlockSpec((1, w), lambda b, g, c: (0, g)),
        pl.BlockSpec((1, w), lambda b, g, c: (0, g)),
    ]
    args += [prm["conv_w"], prm["conv_w"], prm["conv_w"], prm["conv_b"], prm["conv_b"], prm["conv_b"],
             prm["dt_bias_g"], prm["a_log_g"], prm["d_x"], prm["norm_g"]]
    y, h_last = pl.pallas_call(
        functools.partial(_ssd_kernel, lc=lc, valid=valid, has_init=init is not None, hpg=hpg, pdim=pdim),
        grid=(bsz, groups, n_chunks),
        in_specs=in_specs,
        out_specs=[
            pl.BlockSpec((lc, w), lambda b, g, c: (rowblk(b, g, c), g)),
            pl.BlockSpec((1, hpg, pdim, nst), lambda b, g, c: (b, g, 0, 0)),
        ],
        out_shape=[jax.ShapeDtypeStruct((rows, d_inner), BF16),
                   jax.ShapeDtypeStruct((bsz, heads, pdim, nst), F32)],
        scratch_shapes=[pltpu.VMEM((nst, w), F32),
                        pltpu.VMEM((lc + CONV_TAIL, w), F32),
                        pltpu.VMEM((lc + CONV_TAIL, nst), F32),
                        pltpu.VMEM((lc + CONV_TAIL, nst), F32)],
        compiler_params=_cparams("arbitrary", "arbitrary", "arbitrary"),
        name="ssd_scan",
    )(*args)
    return y, h_last


def _attn_kernel(q1_ref, q2_ref, k_ref, v_ref, slope_ref, lam_ref, sn_ref, o_ref,
                 m1_ref, l1_ref, a1_ref, m2_ref, l2_ref, a2_ref, *, tq, lambda_init):
    qi = pl.program_id(2)
    slope = slope_ref[0][:, 0:1]
    rc = (lax.broadcasted_iota(jnp.int32, (tq, tq), 0)
          - lax.broadcasted_iota(jnp.int32, (tq, tq), 1)).astype(F32)
    streams = ((q1_ref, m1_ref, l1_ref, a1_ref), (q2_ref, m2_ref, l2_ref, a2_ref))
    for _, m_ref, l_ref, a_ref in streams:
        m_ref[...] = jnp.full_like(m_ref, -jnp.inf)
        l_ref[...] = jnp.zeros_like(l_ref)
        a_ref[...] = jnp.zeros_like(a_ref)

    def tile(ki, masked):
        start = pl.multiple_of(ki * tq, tq)
        k = k_ref[pl.ds(start, tq), :]
        v = v_ref[pl.ds(start, tq), :]
        if masked:
            bias = slope * rc
        else:
            bias = slope * (rc + ((qi - ki) * tq).astype(F32))
        for q_ref, m_ref, l_ref, a_ref in streams:
            s = _dot_nt(q_ref[...], k) - bias
            if masked:
                s = jnp.where(rc >= 0.0, s, -jnp.inf)
            m_old = m_ref[...]
            m_new = jnp.maximum(m_old, jnp.max(s, axis=-1, keepdims=True))
            alpha = jnp.exp(m_old - m_new)
            p = jnp.exp(s - m_new)
            l_ref[...] = alpha * l_ref[...] + jnp.sum(p, axis=-1, keepdims=True)
            a_ref[...] = alpha * a_ref[...] + _dot(p.astype(BF16), v)
            m_ref[...] = m_new

    def body(ki, carry):
        tile(ki, False)
        return carry

    lax.fori_loop(0, qi, body, 0)
    tile(qi, True)

    lf = lam_ref[...]
    lam = (jnp.exp(jnp.sum(lf[0:1] * lf[1:2], axis=-1, keepdims=True))
           - jnp.exp(jnp.sum(lf[2:3] * lf[3:4], axis=-1, keepdims=True)) + lambda_init)
    o = a1_ref[...] / l1_ref[...] - lam * (a2_ref[...] / l2_ref[...])
    o = o * lax.rsqrt(jnp.mean(o * o, axis=-1, keepdims=True) + NORM_EPS) * sn_ref[...]
    o_ref[...] = (o * (1.0 - lambda_init)).astype(BF16)


def _alibi_slopes(n_heads):
    return jnp.exp2(-8.0 * jnp.arange(1, n_heads + 1, dtype=F32) / n_heads)


def _attn_prompt(q1, q2, kb, vb, bsz, seq, n_heads, lam_p, sub_norm, lambda_init):
    m, width = q1.shape
    hw = width // n_heads
    tq = _pick(seq, (512, 256, 128))
    nq = seq // tq
    slopes = jnp.broadcast_to(_alibi_slopes(n_heads)[:, None, None], (n_heads, 1, LANES))
    return pl.pallas_call(
        functools.partial(_attn_kernel, tq=tq, lambda_init=lambda_init),
        grid=(bsz, n_heads, nq),
        in_specs=[
            pl.BlockSpec((tq, hw), lambda b, h, i: (b * nq + i, h)),
            pl.BlockSpec((tq, hw), lambda b, h, i: (b * nq + i, h)),
            pl.BlockSpec((seq, hw), lambda b, h, i: (b, h)),
            pl.BlockSpec((seq, hw), lambda b, h, i: (b, h)),
            pl.BlockSpec((1, 1, LANES), lambda b, h, i: (h, 0, 0)),
            pl.BlockSpec(lam_p.shape, lambda b, h, i: (0, 0)),
            pl.BlockSpec((1, hw), lambda b, h, i: (0, 0)),
        ],
        out_specs=pl.BlockSpec((tq, hw), lambda b, h, i: (b * nq + i, h)),
        out_shape=jax.ShapeDtypeStruct((m, width), BF16),
        scratch_shapes=[pltpu.VMEM((tq, 1), F32), pltpu.VMEM((tq, 1), F32), pltpu.VMEM((tq, hw), F32),
                        pltpu.VMEM((tq, 1), F32), pltpu.VMEM((tq, 1), F32), pltpu.VMEM((tq, hw), F32)],
        compiler_params=_cparams("parallel", "parallel", "arbitrary"),
        name="attn_prompt",
    )(q1, q2, kb, vb, slopes, lam_p.astype(F32), sub_norm.reshape(1, hw).astype(F32))


def _attn_dec_kernel(pt_ref, q_ref, kp_ref, vp_ref, kn_ref, vn_ref, sl_ref, tq_ref, lam_ref, sn_ref,
                     o_ref, m_ref, l_ref, acc_ref, *, n_pages, page, n_tok, n_heads, hw, past, lambda_init):
    del pt_ref
    pg = pl.program_id(1)
    nrow = 2 * n_tok * n_heads

    @pl.when(pg == 0)
    def _():
        m_ref[...] = jnp.full_like(m_ref, -jnp.inf)
        l_ref[...] = jnp.zeros_like(l_ref)
        acc_ref[...] = jnp.zeros_like(acc_ref)

    q = q_ref[0]
    key = lax.broadcasted_iota(jnp.int32, (nrow, page), 1)

    def update(k, v, kpos, valid):
        dist = tq_ref[...] - kpos.astype(F32)
        s = _dot_nt(q, k.astype(BF16)) - sl_ref[...] * dist
        if valid is not None:
            s = jnp.where(valid & (dist >= 0.0), s, -jnp.inf)
        m_old = m_ref[...]
        m_new = jnp.maximum(m_old, jnp.max(s, axis=-1, keepdims=True))
        alpha = jnp.exp(m_old - m_new)
        p = jnp.exp(s - m_new)
        l_ref[...] = alpha * l_ref[...] + jnp.sum(p, axis=-1, keepdims=True)
        acc_ref[...] = alpha * acc_ref[...] + _dot(p.astype(BF16), v.astype(BF16))
        m_ref[...] = m_new

    @pl.when(pg < n_pages)
    def _():
        update(kp_ref[0, 0], vp_ref[0, 0], pg * page + key, None)

    @pl.when(pg == n_pages)
    def _():
        update(kn_ref[0], vn_ref[0], past + key, key < n_tok)
        lf = lam_ref[...]
        lam = (jnp.exp(jnp.sum(lf[0:1] * lf[1:2], axis=-1, keepdims=True))
               - jnp.exp(jnp.sum(lf[2:3] * lf[3:4], axis=-1, keepdims=True)) + lambda_init)
        half = nrow // 2
        inv_l = 1.0 / l_ref[...]
        o = acc_ref[0:half, :] * inv_l[0:half] - lam * (acc_ref[half:nrow, :] * inv_l[half:nrow])
        hrow = lax.broadcasted_iota(jnp.int32, (n_heads, hw), 0)
        for hb in range(n_heads):
            blk = o[:, hb * hw:(hb + 1) * hw]
            blk = blk * lax.rsqrt(jnp.mean(blk * blk, axis=-1, keepdims=True) + NORM_EPS)
            blk = blk * sn_ref[...] * (1.0 - lambda_init)
            for t in range(n_tok):
                rows = blk[t * n_heads:(t + 1) * n_heads, :]
                o_ref[0, t:t + 1, hb * hw:(hb + 1) * hw] = jnp.sum(
                    jnp.where(hrow == hb, rows, 0.0), axis=0, keepdims=True).astype(o_ref.dtype)


def _attn_sample(q1, q2, k_new, v_new, cache_k, cache_v, layer, page_table, bsz, n_tok, n_heads,
                 lam_p, sub_norm, lambda_init):
    n_phys, page = cache_k.shape[1], cache_k.shape[2]
    hw = cache_k.shape[4]
    width = n_heads * hw
    n_pages = page_table.shape[1]
    past = n_pages * page
    nrow = 2 * n_tok * n_heads
    assert page == LANES and n_tok <= page
    ck = cache_k.reshape(cache_k.shape[0], n_phys, page, width)
    cv = cache_v.reshape(cache_v.shape[0], n_phys, page, width)
    head_mask = (jnp.arange(width)[None, :] // hw == jnp.arange(n_heads)[:, None])
    q12 = jnp.stack([q1, q2]).reshape(2, bsz, n_tok, 1, width)
    qbd = jnp.where(head_mask[None, None, None], q12, jnp.zeros((), q12.dtype))
    qbd = jnp.transpose(qbd, (1, 0, 2, 3, 4)).reshape(bsz, nrow, width)
    pad = ((0, 0), (0, page - n_tok), (0, 0))
    kn = jnp.pad(k_new.reshape(bsz, n_tok, width), pad)
    vn = jnp.pad(v_new.reshape(bsz, n_tok, width), pad)
    r = jnp.arange(nrow)
    slope_rows = jnp.broadcast_to(_alibi_slopes(n_heads)[r % n_heads][:, None], (nrow, page))
    tq_rows = jnp.broadcast_to((past + (r // n_heads) % n_tok).astype(F32)[:, None], (nrow, page))

    def page_idx(b, p, pt):
        return (layer, pt[b, jnp.minimum(p, n_pages - 1)], 0, 0)

    grid_spec = pltpu.PrefetchScalarGridSpec(
        num_scalar_prefetch=1,
        grid=(bsz, n_pages + 1),
        in_specs=[
            pl.BlockSpec((1, nrow, width), lambda b, p, pt: (b, 0, 0)),
            pl.BlockSpec((1, 1, page, width), page_idx),
            pl.BlockSpec((1, 1, page, width), page_idx),
            pl.BlockSpec((1, page, width), lambda b, p, pt: (b, 0, 0)),
            pl.BlockSpec((1, page, width), lambda b, p, pt: (b, 0, 0)),
            pl.BlockSpec((nrow, page), lambda b, p, pt: (0, 0)),
            pl.BlockSpec((nrow, page), lambda b, p, pt: (0, 0)),
            pl.BlockSpec(lam_p.shape, lambda b, p, pt: (0, 0)),
            pl.BlockSpec((1, hw), lambda b, p, pt: (0, 0)),
        ],
        out_specs=pl.BlockSpec((1, n_tok, width), lambda b, p, pt: (b, 0, 0)),
        scratch_shapes=[pltpu.VMEM((nrow, 1), F32), pltpu.VMEM((nrow, 1), F32), pltpu.VMEM((nrow, width), F32)],
    )
    out = pl.pallas_call(
        functools.partial(_attn_dec_kernel, n_pages=n_pages, page=page, n_tok=n_tok, n_heads=n_heads,
                          hw=hw, past=past, lambda_init=lambda_init),
        grid_spec=grid_spec,
        out_shape=jax.ShapeDtypeStruct((bsz, n_tok, width), BF16),
        compiler_params=_cparams("arbitrary", "arbitrary"),
        name="attn_sample",
    )(page_table, qbd, ck, cv, kn, vn, slope_rows, tq_rows, lam_p.astype(F32),
      sub_norm.reshape(1, hw).astype(F32))
    return out.reshape(bsz * n_tok, width)


def _ssd_params(j, dims, ssd_in, conv_w, conv_b, dt_bias, a_log, d_skip, norm_g, w_out):
    d_inner, nst, groups, heads = dims
    hpg = heads // groups
    pdim = d_inner // heads
    conv_dim = conv_w.shape[2]
    d_model = ssd_in.shape[1]

    def per_group(v):
        return jnp.pad(v.astype(F32).reshape(groups, hpg), ((0, 0), (0, LANES - hpg))).reshape(1, groups * LANES)

    w_dt = ssd_in[j][:, d_inner + conv_dim:].reshape(d_model, groups, hpg)
    w_dt = jnp.pad(w_dt, ((0, 0), (0, 0), (0, LANES - hpg))).reshape(d_model, groups * LANES)
    w_all = jnp.concatenate([ssd_in[j][:, :d_inner + conv_dim], w_dt], axis=1).astype(BF16)
    return dict(
        w_in=w_all,
        conv_w=conv_w[j].astype(F32),
        conv_b=conv_b[j].reshape(1, conv_dim).astype(F32),
        dt_bias_g=per_group(dt_bias[j]),
        a_log_g=per_group(a_log[j]),
        d_x=jnp.repeat(d_skip[j].astype(F32), pdim).reshape(1, d_inner),
        norm_g=norm_g[j].reshape(1, d_inner).astype(F32),
        w_out=w_out[j].astype(BF16),
    )


def kernel(x_prompt, x_sample, cache_k, cache_v, state_conv, state_ssm, page_table, p_prompt, p_sample, norm_ffa, ffa_in, ffa_out, norm_mix, norm_ffb, ffb_in, ffb_out, norm_ple, ple_proj, ple_gate, ssd_in, ssd_conv_w, ssd_conv_b, ssd_dt_bias, ssd_a_log, ssd_d, ssd_norm, ssd_out, attn_qkv, attn_qk_norm, attn_lambda, attn_sub_norm, attn_out):
    bp, seq, d_model = x_prompt.shape
    bs, n_tok, _ = x_sample.shape
    depth = norm_ffa.shape[0]
    n_mixers = 2
    heads = ssd_dt_bias.shape[1]
    d_inner = ssd_out.shape[1]
    nst = state_ssm.shape[-1]
    conv_dim = ssd_conv_w.shape[2]
    kw = ssd_conv_w.shape[1]
    groups = (conv_dim - d_inner) // (2 * nst)
    dims = (d_inner, nst, groups, heads)
    n_heads = cache_k.shape[3]
    hw = cache_k.shape[4]
    hd = hw // 2
    att_w = n_heads * hw
    lc = SSD_CHUNK
    assert seq % lc == 0 and n_tok <= lc and n_tok >= kw - 1 and kw - 1 <= CONV_TAIL

    hp = x_prompt.reshape(bp * seq, d_model)
    hs = x_sample.reshape(bs * n_tok, d_model)
    groups_io = [(hp, p_prompt.reshape(depth, bp * seq, -1)), (hs, p_sample.reshape(depth, bs * n_tok, -1))]
    h = [hp, hs]
    ple_in = [groups_io[0][1], groups_io[1][1]]
    k_out, v_out = [[], []], [[], []]
    conv_out, ssm_out = [[], []], [[], []]

    for i in range(depth):
        j = i // n_mixers
        wa_in, wa_out = ffa_in[i].astype(BF16), ffa_out[i].astype(BF16)
        h = [_ffn(x, norm_ffa[i], wa_in, wa_out) for x in h]
        if i % n_mixers == 0:
            prm = _ssd_params(j, dims, ssd_in, ssd_conv_w, ssd_conv_b, ssd_dt_bias, ssd_a_log, ssd_d,
                              ssd_norm, ssd_out)
            zx = _norm_matmul(h[0], norm_mix[i], prm["w_in"])
            y, h_last = _ssd_core(zx, bp, seq // lc, lc, dims, prm)
            conv_out[0].append(zx.reshape(bp, seq, -1)[:, seq - (kw - 1):, d_inner:d_inner + conv_dim])
            ssm_out[0].append(h_last)
            h[0] = _matmul_residual(h[0], y, prm["w_out"])
            zs = _norm_matmul(h[1], norm_mix[i], prm["w_in"])
            zs3 = zs.reshape(bs, n_tok, -1)
            zs_pad = jnp.pad(zs3, ((0, 0), (0, lc - n_tok), (0, 0))).reshape(bs * lc, -1)
            tail = jnp.pad(state_conv[j].astype(F32), ((0, 0), (CONV_TAIL - (kw - 1), 0), (0, 0)))
            y, h_last = _ssd_core(zs_pad, bs, 1, n_tok, dims, prm, init=(tail, state_ssm[j].astype(F32)))
            xbc = zs3[:, :, d_inner:d_inner + conv_dim]
            conv_out[1].append(jnp.concatenate([state_conv[j].astype(F32), xbc], axis=1)[:, n_tok:])
            ssm_out[1].append(h_last)
            y = y.reshape(bs, lc, d_inner)[:, :n_tok].reshape(bs * n_tok, d_inner)
            h[1] = _matmul_residual(h[1], y, prm["w_out"])
        else:
            lambda_init = 0.8 - 0.6 * math.exp(-0.3 * i)
            wq = attn_qkv[j][:, :att_w].astype(BF16)
            wk = attn_qkv[j][:, att_w:2 * att_w].astype(BF16)
            wv = attn_qkv[j][:, 2 * att_w:].astype(BF16)
            w_o = attn_out[j].astype(BF16)
            for gi in range(2):
                q1, q2 = _norm_matmul(h[gi], norm_mix[i], wq, "q", attn_qk_norm[j, 0], hd)
                kf, kb = _norm_matmul(h[gi], norm_mix[i], wk, "k", attn_qk_norm[j, 1], hd)
                vf, vb = _norm_matmul(h[gi], norm_mix[i], wv, "v")
                if gi == 0:
                    o = _attn_prompt(q1, q2, kb, vb, bp, seq, n_heads, attn_lambda[j], attn_sub_norm[j],
                                     lambda_init)
                    k_out[0].append(kf.reshape(bp, seq, n_heads, hw))
                    v_out[0].append(vf.reshape(bp, seq, n_heads, hw))
                else:
                    o = _attn_sample(q1, q2, kf, vf, cache_k, cache_v, j, page_table, bs, n_tok, n_heads,
                                     attn_lambda[j], attn_sub_norm[j], lambda_init)
                    k_out[1].append(kf.reshape(bs, n_tok, n_heads, hw))
                    v_out[1].append(vf.reshape(bs, n_tok, n_heads, hw))
                h[gi] = _matmul_residual(h[gi], o, w_o)
        wb_in, wb_out = ffb_in[i].astype(BF16), ffb_out[i].astype(BF16)
        h = [_ffn(x, norm_ffb[i], wb_in, wb_out) for x in h]
        wp, wg = ple_proj[i].astype(BF16), ple_gate[i].astype(BF16)
        h = [_ple(x, ple_in[gi][i], norm_ple[i], wp, wg) for gi, x in enumerate(h)]

    return (h[0].reshape(bp, seq, d_model), h[1].reshape(bs, n_tok, d_model),
            jnp.stack(k_out[0]), jnp.stack(v_out[0]), jnp.stack(k_out[1]), jnp.stack(v_out[1]),
            jnp.stack(conv_out[0]), jnp.stack(ssm_out[0]), jnp.stack(conv_out[1]), jnp.stack(ssm_out[1]))
```

```python
import functools
import math

import jax
import jax.numpy as jnp
from jax import lax
from jax.experimental import pallas as pl
from jax.experimental.pallas import tpu as pltpu

F32 = jnp.float32
BF16 = jnp.bfloat16
NORM_EPS = 1e-6
LOG2E = 1.4426950408889634
SSD_CHUNK = 128
LANES = 128
CONV_TAIL = 8
VMEM_LIMIT_BYTES = 56 * 1024 * 1024
ROW_TILES = (512, 256, 128, 64, 32, 16, 8)


def _cparams(*sem):
    return pltpu.CompilerParams(dimension_semantics=sem, vmem_limit_bytes=VMEM_LIMIT_BYTES)


def _pick(n, prefs):
    for t in prefs:
        if n % t == 0:
            return t
    return n


def _silu(x):
    return x * jax.nn.sigmoid(x)


def _rms(x, g):
    return x * lax.rsqrt(jnp.mean(x * x, axis=-1, keepdims=True) + NORM_EPS) * g


def _dot(a, b):
    return jnp.dot(a, b, preferred_element_type=F32)


def _dot_nt(a, b):
    return lax.dot_general(a, b, (((1,), (1,)), ((), ())), preferred_element_type=F32)


def _split3(x):
    hi = x.astype(BF16)
    r = x - hi.astype(F32)
    mid = r.astype(BF16)
    lo = (r - mid.astype(F32)).astype(BF16)
    return hi, mid, lo


def _lambda_full(lam_ref, lambda_init):
    lf = lam_ref[...]
    return (jnp.exp(jnp.sum(lf[0:1] * lf[1:2], axis=-1, keepdims=True))
            - jnp.exp(jnp.sum(lf[2:3] * lf[3:4], axis=-1, keepdims=True)) + lambda_init)


def _ffn_kernel(h_ref, g_ref, wa_ref, wb_ref, wo_ref, o_ref, xn_ref, acc_ref):
    f = pl.program_id(1)

    @pl.when(f == 0)
    def _():
        xn_ref[...] = _rms(h_ref[...], g_ref[...]).astype(BF16)
        acc_ref[...] = jnp.zeros_like(acc_ref)

    xn = xn_ref[...]
    a = _dot(xn, wa_ref[...])
    b = _dot(xn, wb_ref[...])
    acc_ref[...] += _dot((_silu(a) * b).astype(BF16), wo_ref[...])

    @pl.when(f == pl.num_programs(1) - 1)
    def _():
        o_ref[...] = h_ref[...] + 0.5 * acc_ref[...]


def _ffn(h, g, w_in, w_out, layer):
    m, d = h.shape
    dff = w_out.shape[1]
    tm = _pick(m, ROW_TILES)
    tf = _pick(dff, (512, 256, 128))
    nf = dff // tf
    return pl.pallas_call(
        _ffn_kernel,
        grid=(m // tm, nf),
        in_specs=[
            pl.BlockSpec((tm, d), lambda i, f: (i, 0)),
            pl.BlockSpec((1, d), lambda i, f: (0, 0)),
            pl.BlockSpec((None, d, tf), lambda i, f: (layer, 0, f)),
            pl.BlockSpec((None, d, tf), lambda i, f: (layer, 0, f + nf)),
            pl.BlockSpec((None, tf, d), lambda i, f: (layer, f, 0)),
        ],
        out_specs=pl.BlockSpec((tm, d), lambda i, f: (i, 0)),
        out_shape=jax.ShapeDtypeStruct((m, d), F32),
        scratch_shapes=[pltpu.VMEM((tm, d), BF16), pltpu.VMEM((tm, d), F32)],
        compiler_params=_cparams("parallel", "arbitrary"),
        name="ffn",
    )(h, g.reshape(1, d), w_in, w_in, w_out)


def _ple_kernel(h_ref, p_ref, g_ref, wp_ref, wg_ref, o_ref):
    h = h_ref[...]
    xn = _rms(h, g_ref[...]).astype(BF16)
    gate = jax.nn.sigmoid(_dot(xn, wg_ref[...]))
    proj = _dot(p_ref[...].astype(BF16), wp_ref[...])
    o_ref[...] = h + proj * gate


def _ple(h, p, g, w_proj, w_gate, layer):
    m, d = h.shape
    pd = p.shape[1]
    tm = _pick(m, (256, 128, 64, 32, 16, 8))
    return pl.pallas_call(
        _ple_kernel,
        grid=(m // tm,),
        in_specs=[
            pl.BlockSpec((tm, d), lambda i: (i, 0)),
            pl.BlockSpec((tm, pd), lambda i: (i, 0)),
            pl.BlockSpec((1, d), lambda i: (0, 0)),
            pl.BlockSpec((None, pd, d), lambda i: (layer, 0, 0)),
            pl.BlockSpec((None, d, d), lambda i: (layer, 0, 0)),
        ],
        out_specs=pl.BlockSpec((tm, d), lambda i: (i, 0)),
        out_shape=jax.ShapeDtypeStruct((m, d), F32),
        compiler_params=_cparams("parallel"),
        name="ple",
    )(h, p, g.reshape(1, d), w_proj, w_gate)


def _seg_meansq(y, seg_ref, hd):
    y2 = y * y
    hi = y2.astype(BF16)
    lo = (y2 - hi.astype(F32)).astype(BF16)
    seg = seg_ref[...]
    return (_dot(hi, seg) + _dot(lo, seg)) * (1.0 / hd)


def _nmm_kernel(*refs, mode, hd, has_extra, has_prev):
    h_ref, g_ref, w_ref = refs[:3]
    pos = 3
    if mode in ("q", "k"):
        gn_ref, seg_ref = refs[pos:pos + 2]
        pos += 2
    if has_extra:
        wx_ref = refs[pos]
        pos += 1
    if has_prev:
        pos += 1
    outs = refs[pos:-1]
    xn_ref = refs[-1]

    @pl.when(pl.program_id(1) == 0)
    def _():
        xn_ref[...] = _rms(h_ref[...], g_ref[...]).astype(BF16)
        if has_extra:
            outs[1][...] = _dot(xn_ref[...], wx_ref[...])

    y = _dot(xn_ref[...], w_ref[...])
    if mode in ("plain", "v"):
        outs[0][...] = y
    elif mode == "vt":
        outs[0][...] = y
        outs[1][...] = y.T.astype(BF16)
    else:
        yn = y * lax.rsqrt(_seg_meansq(y, seg_ref, hd) + NORM_EPS) * gn_ref[...]
        if mode == "k":
            outs[0][...] = yn
            outs[1][...] = yn.astype(BF16)
        else:
            yn = yn * (LOG2E / math.sqrt(hd))
            lane = lax.broadcasted_iota(jnp.int32, yn.shape, 1)
            first = (lane % (2 * hd)) < hd
            outs[0][...] = jnp.where(first, yn, 0.0).astype(BF16)
            outs[1][...] = jnp.where(first, 0.0, yn).astype(BF16)


def _norm_matmul(h, g, w, layer, col0, n, mode="plain", gain=None, hd=None, extra=None, stack=None, tm=None):
    m, d = h.shape
    tm = tm or _pick(m, ROW_TILES)
    tn = _pick(math.gcd(n, col0) if col0 else n,
               (256, 128) if mode in ("q", "k") else (1024, 512, 256, 128))
    cb0 = col0 // tn
    in_specs = [
        pl.BlockSpec((tm, d), lambda i, j: (i, 0)),
        pl.BlockSpec((1, d), lambda i, j: (0, 0)),
        pl.BlockSpec((None, d, tn), lambda i, j: (layer, 0, cb0 + j)),
    ]
    args = [h, g.reshape(1, d), w]
    if mode in ("q", "k"):
        gain_row = jnp.tile(gain.reshape(1, 2 * hd).astype(F32), (1, tn // (2 * hd)))
        lane = jnp.arange(tn)
        seg = (lane[:, None] // hd == lane[None, :] // hd).astype(BF16)
        in_specs += [pl.BlockSpec((1, tn), lambda i, j: (0, 0)),
                     pl.BlockSpec((tn, tn), lambda i, j: (0, 0))]
        args += [gain_row, seg]
    if extra is not None:
        nx = extra.shape[1]
        in_specs.append(pl.BlockSpec((d, nx), lambda i, j: (0, 0)))
        args.append(extra)

    tile_spec = pl.BlockSpec((tm, tn), lambda i, j: (i, j))
    if stack is not None:
        slot, slots, prev = stack
        first_spec = pl.BlockSpec((None, tm, tn), lambda i, j: (slot, i, j))
        first_shape = jax.ShapeDtypeStruct((slots, m, n), F32)
    else:
        prev = None
        first_spec, first_shape = tile_spec, jax.ShapeDtypeStruct((m, n), F32)
    aliases = {}
    if prev is not None:
        aliases = {len(args): 0}
        in_specs.append(pl.BlockSpec(memory_space=pl.ANY))
        args.append(prev)

    if mode == "q":
        out_specs = [tile_spec, tile_spec]
        out_shape = [jax.ShapeDtypeStruct((m, n), BF16)] * 2
    elif mode == "k":
        out_specs = [first_spec, tile_spec]
        out_shape = [first_shape, jax.ShapeDtypeStruct((m, n), BF16)]
    elif mode == "vt":
        out_specs = [first_spec, pl.BlockSpec((None, tn, tm), lambda i, j: (i, j, 0))]
        out_shape = [first_shape, jax.ShapeDtypeStruct((m // tm, n, tm), BF16)]
    else:
        out_specs, out_shape = [first_spec], [first_shape]
    if extra is not None:
        out_specs.append(pl.BlockSpec((tm, nx), lambda i, j: (i, 0)))
        out_shape.append(jax.ShapeDtypeStruct((m, nx), F32))
    out = pl.pallas_call(
        functools.partial(_nmm_kernel, mode=mode, hd=hd, has_extra=extra is not None,
                          has_prev=prev is not None),
        grid=(m // tm, n // tn),
        in_specs=in_specs,
        out_specs=out_specs,
        out_shape=out_shape,
        scratch_shapes=[pltpu.VMEM((tm, d), BF16)],
        input_output_aliases=aliases,
        compiler_params=_cparams("parallel", "arbitrary"),
        name="norm_matmul_" + mode,
    )(*args)
    return out[0] if len(out) == 1 else out


def _mm_res_kernel(h_ref, x_ref, w_ref, o_ref):
    o_ref[...] = h_ref[...] + _dot(x_ref[...].astype(BF16), w_ref[...])


def _matmul_residual(h, x, w, layer):
    m, n = h.shape
    k = x.shape[1]
    tm = _pick(m, ROW_TILES)
    tn = _pick(n, (512, 256, 128))
    return pl.pallas_call(
        _mm_res_kernel,
        grid=(m // tm, n // tn),
        in_specs=[
            pl.BlockSpec((tm, tn), lambda i, j: (i, j)),
            pl.BlockSpec((tm, k), lambda i, j: (i, 0)),
            pl.BlockSpec((None, k, tn), lambda i, j: (layer, 0, j)),
        ],
        out_specs=pl.BlockSpec((tm, tn), lambda i, j: (i, j)),
        out_shape=jax.ShapeDtypeStruct((m, n), F32),
        compiler_params=_cparams("parallel", "arbitrary"),
        name="matmul_residual",
    )(h, x, w)


def _ssd_kernel(*refs, lc, valid, has_init, hpg, pdim):
    (z_ref, x_ref, b_ref, c_ref, dt_ref) = refs[:5]
    pos = 5
    if has_init:
        tx_ref, tb_ref, tc_ref, h0_ref = refs[pos:pos + 4]
        pos += 4
    (cwx_ref, cwb_ref, cwc_ref, cbx_ref, cbb_ref, cbc_ref,
     dtb_ref, alog_ref, dsk_ref, ng_ref) = refs[pos:pos + 10]
    pos += 10
    y_ref, hout_ref = refs[pos:pos + 2]
    ht_ref, ex_ref, eb_ref, ec_ref = refs[pos + 2:]

    c = pl.program_id(2)
    w = hpg * pdim
    nst = ht_ref.shape[0]
    t0 = CONV_TAIL

    @pl.when(c == 0)
    def _():
        if has_init:
            ex_ref[0:t0, :] = tx_ref[0]
            eb_ref[0:t0, :] = tb_ref[0]
            ec_ref[0:t0, :] = tc_ref[0]
            ht_ref[...] = h0_ref[0].reshape(w, nst).T
        else:
            ex_ref[0:t0, :] = jnp.zeros((t0, w), F32)
            eb_ref[0:t0, :] = jnp.zeros((t0, nst), F32)
            ec_ref[0:t0, :] = jnp.zeros((t0, nst), F32)
            ht_ref[...] = jnp.zeros_like(ht_ref)

    def conv_act(e_ref, raw_ref, cw_ref, cb_ref):
        e_ref[t0:t0 + lc, :] = raw_ref[...]
        kw = cw_ref.shape[0]
        acc = cb_ref[...] + cw_ref[kw - 1:kw, :] * e_ref[t0:t0 + lc, :]
        for k in range(kw - 1):
            off = t0 - (kw - 1) + k
            acc = acc + cw_ref[k:k + 1, :] * e_ref[off:off + lc, :]
        e_ref[0:t0, :] = e_ref[lc:lc + t0, :]
        return _silu(acc)

    xc = conv_act(ex_ref, x_ref, cwx_ref, cbx_ref)
    bc = conv_act(eb_ref, b_ref, cwb_ref, cbb_ref)
    cc = conv_act(ec_ref, c_ref, cwc_ref, cbc_ref)

    row = lax.broadcasted_iota(jnp.int32, (lc, lc), 0)
    col = lax.broadcasted_iota(jnp.int32, (lc, lc), 1)
    causal = row >= col
    lane = lax.broadcasted_iota(jnp.int32, (lc, LANES), 1)

    dtr = dt_ref[...] + dtb_ref[...]
    dt = jnp.maximum(dtr, 0.0) + jnp.log1p(jnp.exp(-jnp.abs(dtr)))
    if valid < lc:
        dt = jnp.where(lax.broadcasted_iota(jnp.int32, (lc, LANES), 0) < valid, dt, 0.0)
    da = dt * (-jnp.exp(alog_ref[...]))
    tri = causal.astype(BF16)
    d_hi, d_mid, d_lo = _split3(da)
    cs = _dot(tri, d_hi) + _dot(tri, d_mid) + _dot(tri, d_lo)
    cs_t = cs.T
    ecs = jnp.exp(cs)
    coef_s = dt * jnp.exp(cs[lc - 1:lc, :] - cs)

    def head_col(arr, j):
        return jnp.sum(jnp.where(lane == j, arr, 0.0), axis=-1, keepdims=True)

    first_half = lane < pdim

    def expand(arr):
        blks = [jnp.where(first_half, head_col(arr, 2 * q), head_col(arr, 2 * q + 1))
                for q in range(hpg // 2)]
        return blks[0] if len(blks) == 1 else jnp.concatenate(blks, axis=1)

    dt_x = expand(dt)
    coef_x = expand(coef_s)
    ecs_x = expand(ecs)

    bcb = bc.astype(BF16)
    ccb = cc.astype(BF16)
    cb = _dot_nt(ccb, bcb)
    xd = xc * dt_x

    def decay_mat(j):
        seg = head_col(cs, j) - cs_t[j:j + 1, :]
        return (cb * jnp.exp(jnp.where(causal, seg, -jnp.inf))).astype(BF16)

    y_blks = []
    for q in range(hpg // 2):
        mcat = jnp.concatenate([decay_mat(2 * q), decay_mat(2 * q + 1)], axis=1)
        xp = xd[:, q * LANES:(q + 1) * LANES]
        rhs = jnp.concatenate([jnp.where(first_half, xp, 0.0), jnp.where(first_half, 0.0, xp)],
                              axis=0).astype(BF16)
        y_blks.append(_dot(mcat, rhs))
    y = y_blks[0] if len(y_blks) == 1 else jnp.concatenate(y_blks, axis=1)

    ht_in = ht_ref[...]
    y = y + _dot(ccb, ht_in.astype(BF16)) * ecs_x
    upd = _dot(bc.T.astype(BF16), (xc * coef_x).astype(BF16))
    ht_ref[...] = ht_in * ecs_x[lc - 1:lc, :] + upd

    y = y + xc * dsk_ref[...]
    y = y * _silu(z_ref[...])
    y = y * lax.rsqrt(jnp.mean(y * y, axis=-1, keepdims=True) + NORM_EPS) * ng_ref[...]
    y_ref[...] = y.astype(BF16)

    @pl.when(c == pl.num_programs(2) - 1)
    def _():
        hout_ref[0] = ht_ref[...].T.reshape(hpg, pdim, nst)


def _ssd_core(zx, dtg, bsz, n_chunks, valid, dims, prm, init=None):
    d_inner, nst, groups, heads = dims
    hpg = heads // groups
    pdim = d_inner // heads
    w = hpg * pdim
    lc = SSD_CHUNK
    assert pdim * 2 == LANES and hpg % 2 == 0 and nst == LANES
    rows = bsz * n_chunks * lc
    xo, bo, co = d_inner // w, 2 * d_inner // nst, (2 * d_inner + groups * nst) // nst
    cbo, cco = d_inner // nst, (d_inner + groups * nst) // nst

    def rowblk(b, g, c):
        return b * n_chunks + c

    in_specs = [
        pl.BlockSpec((lc, w), lambda b, g, c: (rowblk(b, g, c), g)),
        pl.BlockSpec((lc, w), lambda b, g, c: (rowblk(b, g, c), xo + g)),
        pl.BlockSpec((lc, nst), lambda b, g, c: (rowblk(b, g, c), bo + g)),
        pl.BlockSpec((lc, nst), lambda b, g, c: (rowblk(b, g, c), co + g)),
        pl.BlockSpec((lc, LANES), lambda b, g, c: (rowblk(b, g, c), g)),
    ]
    args = [zx, zx, zx, zx, dtg]
    if init is not None:
        tail, h0 = init
        in_specs += [
            pl.BlockSpec((1, CONV_TAIL, w), lambda b, g, c: (b, 0, g)),
            pl.BlockSpec((1, CONV_TAIL, nst), lambda b, g, c: (b, 0, cbo + g)),
            pl.BlockSpec((1, CONV_TAIL, nst), lambda b, g, c: (b, 0, cco + g)),
            pl.BlockSpec((1, hpg, pdim, nst), lambda b, g, c: (b, g, 0, 0)),
        ]
        args += [tail, tail, tail, h0]
    kw = prm["conv_w"].shape[0]
    in_specs += [
        pl.BlockSpec((kw, w), lambda b, g, c: (0, g)),
        pl.BlockSpec((kw, nst), lambda b, g, c: (0, cbo + g)),
        pl.BlockSpec((kw, nst), lambda b, g, c: (0, cco + g)),
        pl.BlockSpec((1, w), lambda b, g, c: (0, g)),
        pl.BlockSpec((1, nst), lambda b, g, c: (0, cbo + g)),
        pl.BlockSpec((1, nst), lambda b, g, c: (0, cco + g)),
        pl.BlockSpec((1, LANES), lambda b, g, c: (0, g)),
        pl.BlockSpec((1, LANES), lambda b, g, c: (0, g)),
        pl.BlockSpec((1, w), lambda b, g, c: (0, g)),
        pl.BlockSpec((1, w), lambda b, g, c: (0, g)),
    ]
    args += [prm["conv_w"], prm["conv_w"], prm["conv_w"], prm["conv_b"], prm["conv_b"], prm["conv_b"],
             prm["dt_bias_g"], prm["a_log_g"], prm["d_x"], prm["norm_g"]]
    y, h_last = pl.pallas_call(
        functools.partial(_ssd_kernel, lc=lc, valid=valid, has_init=init is not None, hpg=hpg, pdim=pdim),
        grid=(bsz, groups, n_chunks),
        in_specs=in_specs,
        out_specs=[
            pl.BlockSpec((lc, w), lambda b, g, c: (rowblk(b, g, c), g)),
            pl.BlockSpec((1, hpg, pdim, nst), lambda b, g, c: (b, g, 0, 0)),
        ],
        out_shape=[jax.ShapeDtypeStruct((rows, d_inner), BF16),
                   jax.ShapeDtypeStruct((bsz, heads, pdim, nst), F32)],
        scratch_shapes=[pltpu.VMEM((nst, w), F32),
                        pltpu.VMEM((lc + CONV_TAIL, w), F32),
                        pltpu.VMEM((lc + CONV_TAIL, nst), F32),
                        pltpu.VMEM((lc + CONV_TAIL, nst), F32)],
        compiler_params=_cparams("arbitrary", "arbitrary", "arbitrary"),
        name="ssd_scan",
    )(*args)
    return y, h_last


def _attn_kernel(q1_ref, q2_ref, k_ref, vt_ref, slope_ref, lam_ref, sn_ref, o_ref,
                 m1_ref, l1_ref, a1_ref, m2_ref, l2_ref, a2_ref, *, tq, lambda_init):
    qi = pl.program_id(2)
    hw = k_ref.shape[1]
    slope2 = slope_ref[0][:, 0:1] * LOG2E
    key_row = lax.broadcasted_iota(jnp.int32, (tq, LANES), 0).astype(F32)
    streams = ((q1_ref, m1_ref, l1_ref, a1_ref), (q2_ref, m2_ref, l2_ref, a2_ref))
    for _, m_ref, l_ref, a_ref in streams:
        m_ref[...] = jnp.full_like(m_ref, -jnp.inf)
        l_ref[...] = jnp.zeros_like(l_ref)
        a_ref[...] = jnp.zeros_like(a_ref)

    def tile(ki, masked):
        start = pl.multiple_of(ki * tq, tq)
        k = k_ref[pl.ds(start, tq), :]
        vt = vt_ref[ki]
        kb = slope2 * (key_row + start.astype(F32))
        kb = jnp.concatenate([kb] * (tq // LANES), axis=1)
        if masked:
            keep = (lax.broadcasted_iota(jnp.int32, (tq, tq), 0)
                    <= lax.broadcasted_iota(jnp.int32, (tq, tq), 1))
        for q_ref, m_ref, l_ref, a_ref in streams:
            s = _dot_nt(k, q_ref[...]) + kb
            if masked:
                s = jnp.where(keep, s, -jnp.inf)
            m_old = m_ref[...]
            m_new = jnp.maximum(m_old, jnp.max(s, axis=0, keepdims=True))
            alpha = jnp.exp2(m_old - m_new)
            p = jnp.exp2(s - m_new)
            l_ref[...] = alpha * l_ref[...] + jnp.sum(p, axis=0, keepdims=True)
            a_ref[...] = alpha * a_ref[...] + _dot(vt, p.astype(BF16))
            m_ref[...] = m_new

    def body(ki, carry):
        tile(ki, False)
        return carry

    lax.fori_loop(0, qi, body, 0)
    tile(qi, True)

    lam = _lambda_full(lam_ref, lambda_init)
    ot = a1_ref[...] / l1_ref[...] - lam * (a2_ref[...] / l2_ref[...])
    ot = ot * lax.rsqrt(jnp.mean(ot * ot, axis=0, keepdims=True) + NORM_EPS)
    o_ref[...] = (ot.T * (sn_ref[...] * (1.0 - lambda_init))).astype(BF16)


def _alibi_slopes(n_heads):
    return jnp.exp2(-8.0 * jnp.arange(1, n_heads + 1, dtype=F32) / n_heads)


def _attn_prompt(q1, q2, kb, vt, bsz, seq, n_heads, lam_p, sub_norm, lambda_init):
    m, width = q1.shape
    hw = width // n_heads
    tq = vt.shape[2]
    nq = seq // tq
    assert seq % tq == 0 and tq % LANES == 0
    slopes = jnp.broadcast_to(_alibi_slopes(n_heads)[:, None, None], (n_heads, 1, LANES))
    return pl.pallas_call(
        functools.partial(_attn_kernel, tq=tq, lambda_init=lambda_init),
        grid=(bsz, n_heads, nq),
        in_specs=[
            pl.BlockSpec((tq, hw), lambda b, h, i: (b * nq + i, h)),
            pl.BlockSpec((tq, hw), lambda b, h, i: (b * nq + i, h)),
            pl.BlockSpec((seq, hw), lambda b, h, i: (b, h)),
            pl.BlockSpec((nq, hw, tq), lambda b, h, i: (b, h, 0)),
            pl.BlockSpec((1, 1, LANES), lambda b, h, i: (h, 0, 0)),
            pl.BlockSpec(lam_p.shape, lambda b, h, i: (0, 0)),
            pl.BlockSpec((1, hw), lambda b, h, i: (0, 0)),
        ],
        out_specs=pl.BlockSpec((tq, hw), lambda b, h, i: (b * nq + i, h)),
        out_shape=jax.ShapeDtypeStruct((m, width), BF16),
        scratch_shapes=[pltpu.VMEM((1, tq), F32), pltpu.VMEM((1, tq), F32), pltpu.VMEM((hw, tq), F32),
                        pltpu.VMEM((1, tq), F32), pltpu.VMEM((1, tq), F32), pltpu.VMEM((hw, tq), F32)],
        compiler_params=_cparams("parallel", "parallel", "arbitrary"),
        name="attn_prompt",
    )(q1, q2, kb, vt, slopes, lam_p.astype(F32), sub_norm.reshape(1, hw).astype(F32))


def _attn_dec_kernel(pt_ref, q_ref, kp_ref, vp_ref, kn_ref, vn_ref, base_ref, basen_ref, slc_ref, lam_ref,
                     sn_ref, o_ref, m_ref, l_ref, acc_ref, *, n_pages, n_tok, n_heads, lambda_init):
    del pt_ref
    pg = pl.program_id(1)
    rph = 2 * n_tok

    @pl.when(pg == 0)
    def _():
        m_ref[...] = jnp.full_like(m_ref, -jnp.inf)
        l_ref[...] = jnp.zeros_like(l_ref)
        acc_ref[...] = jnp.zeros_like(acc_ref)

    q = q_ref[0]

    def update(k3, v3, bias_ref, shift):
        rows = k3.shape[0] * k3.shape[1]
        k = k3.reshape(rows, k3.shape[2]).astype(BF16)
        v = v3.reshape(rows, v3.shape[2]).astype(BF16)
        s = _dot_nt(q, k) + bias_ref[...]
        m_old = m_ref[...]
        m_new = jnp.maximum(m_old, jnp.max(s, axis=-1, keepdims=True) + shift)
        alpha = jnp.exp2(m_old - m_new)
        p = jnp.exp2(s - (m_new - shift))
        l_ref[...] = alpha * l_ref[...] + jnp.sum(p, axis=-1, keepdims=True)
        acc_ref[...] = alpha * acc_ref[...] + _dot(p.astype(BF16), v)
        m_ref[...] = m_new

    @pl.when(pg < n_pages)
    def _():
        update(kp_ref[0, 0], vp_ref[0, 0], base_ref, slc_ref[...] * pg.astype(F32))

    @pl.when(pg == n_pages)
    def _():
        update(kn_ref[0], vn_ref[0], basen_ref, 0.0)
        lam = _lambda_full(lam_ref, lambda_init)
        on = acc_ref[...] / l_ref[...]
        hw = on.shape[1]
        for h in range(n_heads):
            o = on[h * rph:h * rph + n_tok, :] - lam * on[h * rph + n_tok:(h + 1) * rph, :]
            o = o * lax.rsqrt(jnp.mean(o * o, axis=-1, keepdims=True) + NORM_EPS)
            o_ref[0, :, h * hw:(h + 1) * hw] = (o * (sn_ref[...] * (1.0 - lambda_init))).astype(o_ref.dtype)


def _attn_sample(q1, q2, k_new, v_new, cache_k, cache_v, layer, page_table, bsz, n_tok, n_heads,
                 lam_p, sub_norm, lambda_init):
    page, hw = cache_k.shape[2], cache_k.shape[4]
    n_pages = page_table.shape[1]
    past = n_pages * page
    rph = 2 * n_tok
    nrow = rph * n_heads
    ncol = page * n_heads
    assert n_tok <= page and rph % 8 == 0 and n_heads % 8 == 0
    q12 = jnp.stack([q1, q2]).reshape(2, bsz, n_tok, n_heads, hw)
    qs = jnp.transpose(q12, (1, 3, 0, 2, 4)).reshape(bsz, nrow, hw)
    pad = ((0, 0), (0, page - n_tok), (0, 0), (0, 0))
    kn = jnp.pad(k_new.reshape(bsz, n_tok, n_heads, hw), pad)
    vn = jnp.pad(v_new.reshape(bsz, n_tok, n_heads, hw), pad)
    r = jnp.arange(nrow)
    c = jnp.arange(ncol)
    row_head, row_tok = r // rph, (r % n_tok).astype(F32)
    col_key, col_head = (c // n_heads).astype(F32), c % n_heads
    slope2 = (_alibi_slopes(n_heads) * LOG2E)[row_head]
    same_head = row_head[:, None] == col_head[None, :]
    dist_page = past + row_tok[:, None] - col_key[None, :]
    base = jnp.where(same_head, -slope2[:, None] * dist_page, -jnp.inf)
    dist_new = row_tok[:, None] - col_key[None, :]
    ok_new = same_head & (col_key[None, :] < n_tok) & (dist_new >= 0)
    base_new = jnp.where(ok_new, -slope2[:, None] * dist_new, -jnp.inf)
    slope_page = (slope2 * page).reshape(nrow, 1)

    def page_idx(b, p, pt):
        return (layer, pt[b, jnp.minimum(p, n_pages - 1)], 0, 0, 0)

    def const2(b, p, pt):
        return (0, 0)

    grid_spec = pltpu.PrefetchScalarGridSpec(
        num_scalar_prefetch=1,
        grid=(bsz, n_pages + 1),
        in_specs=[
            pl.BlockSpec((1, nrow, hw), lambda b, p, pt: (b, 0, 0)),
            pl.BlockSpec((1, 1, page, n_heads, hw), page_idx),
            pl.BlockSpec((1, 1, page, n_heads, hw), page_idx),
            pl.BlockSpec((1, page, n_heads, hw), lambda b, p, pt: (b, 0, 0, 0)),
            pl.BlockSpec((1, page, n_heads, hw), lambda b, p, pt: (b, 0, 0, 0)),
            pl.BlockSpec((nrow, ncol), const2),
            pl.BlockSpec((nrow, ncol), const2),
            pl.BlockSpec((nrow, 1), const2),
            pl.BlockSpec(lam_p.shape, const2),
            pl.BlockSpec((1, hw), const2),
        ],
        out_specs=pl.BlockSpec((1, n_tok, n_heads * hw), lambda b, p, pt: (b, 0, 0)),
        scratch_shapes=[pltpu.VMEM((nrow, 1), F32), pltpu.VMEM((nrow, 1), F32), pltpu.VMEM((nrow, hw), F32)],
    )
    out = pl.pallas_call(
        functools.partial(_attn_dec_kernel, n_pages=n_pages, n_tok=n_tok, n_heads=n_heads,
                          lambda_init=lambda_init),
        grid_spec=grid_spec,
        out_shape=jax.ShapeDtypeStruct((bsz, n_tok, n_heads * hw), BF16),
        compiler_params=_cparams("arbitrary", "arbitrary"),
        name="attn_sample",
    )(page_table, qs, cache_k, cache_v, kn, vn, base, base_new, slope_page, lam_p.astype(F32),
      sub_norm.reshape(1, hw).astype(F32))
    return out.reshape(bsz * n_tok, n_heads * hw)


def _ssd_params(j, dims, ssd_in, conv_w, conv_b, dt_bias, a_log, d_skip, norm_g):
    d_inner, nst, groups, heads = dims
    hpg = heads // groups
    pdim = d_inner // heads
    conv_dim = conv_w.shape[2]
    d_model = ssd_in.shape[1]

    def per_group(v):
        return jnp.pad(v.astype(F32).reshape(groups, hpg), ((0, 0), (0, LANES - hpg))).reshape(1, groups * LANES)

    w_dt = ssd_in[j][:, d_inner + conv_dim:].reshape(d_model, groups, hpg)
    w_dt = jnp.pad(w_dt, ((0, 0), (0, 0), (0, LANES - hpg))).reshape(d_model, groups * LANES)
    return dict(
        w_dt=w_dt.astype(BF16),
        conv_w=conv_w[j].astype(F32),
        conv_b=conv_b[j].reshape(1, conv_dim).astype(F32),
        dt_bias_g=per_group(dt_bias[j]),
        a_log_g=per_group(a_log[j]),
        d_x=jnp.repeat(d_skip[j].astype(F32), pdim).reshape(1, d_inner),
        norm_g=norm_g[j].reshape(1, d_inner).astype(F32),
    )


def kernel(x_prompt, x_sample, cache_k, cache_v, state_conv, state_ssm, page_table, p_prompt, p_sample, norm_ffa, ffa_in, ffa_out, norm_mix, norm_ffb, ffb_in, ffb_out, norm_ple, ple_proj, ple_gate, ssd_in, ssd_conv_w, ssd_conv_b, ssd_dt_bias, ssd_a_log, ssd_d, ssd_norm, ssd_out, attn_qkv, attn_qk_norm, attn_lambda, attn_sub_norm, attn_out):
    bp, seq, d_model = x_prompt.shape
    bs, n_tok, _ = x_sample.shape
    depth = norm_ffa.shape[0]
    n_mixers = 2
    n_attn = depth // n_mixers
    heads = ssd_dt_bias.shape[1]
    d_inner = ssd_out.shape[1]
    nst = state_ssm.shape[-1]
    conv_dim = ssd_conv_w.shape[2]
    kw = ssd_conv_w.shape[1]
    groups = (conv_dim - d_inner) // (2 * nst)
    dims = (d_inner, nst, groups, heads)
    n_heads = cache_k.shape[3]
    hw = cache_k.shape[4]
    hd = hw // 2
    att_w = n_heads * hw
    lc = SSD_CHUNK
    assert seq % lc == 0 and n_tok <= lc and n_tok >= kw - 1 and kw - 1 <= CONV_TAIL

    ffa_in_b, ffa_out_b = ffa_in.astype(BF16), ffa_out.astype(BF16)
    ffb_in_b, ffb_out_b = ffb_in.astype(BF16), ffb_out.astype(BF16)
    ple_proj_b, ple_gate_b = ple_proj.astype(BF16), ple_gate.astype(BF16)
    ssd_in_b, ssd_out_b = ssd_in.astype(BF16), ssd_out.astype(BF16)
    attn_qkv_b, attn_out_b = attn_qkv.astype(BF16), attn_out.astype(BF16)

    h = [x_prompt.reshape(bp * seq, d_model), x_sample.reshape(bs * n_tok, d_model)]
    ple_in = [p_prompt.reshape(depth, bp * seq, -1), p_sample.reshape(depth, bs * n_tok, -1)]
    kp_stack = vp_stack = None
    ks_l, vs_l = [], []
    conv_out, ssm_out = [[], []], [[], []]

    for i in range(depth):
        j = i // n_mixers
        h = [_ffn(x, norm_ffa[i], ffa_in_b, ffa_out_b, i) for x in h]
        if i % n_mixers == 0:
            prm = _ssd_params(j, dims, ssd_in, ssd_conv_w, ssd_conv_b, ssd_dt_bias, ssd_a_log, ssd_d, ssd_norm)
            n_zx = d_inner + conv_dim
            zx, dtg = _norm_matmul(h[0], norm_mix[i], ssd_in_b, j, 0, n_zx, extra=prm["w_dt"])
            y, h_last = _ssd_core(zx, dtg, bp, seq // lc, lc, dims, prm)
            conv_out[0].append(zx.reshape(bp, seq, -1)[:, seq - (kw - 1):, d_inner:])
            ssm_out[0].append(h_last)
            h[0] = _matmul_residual(h[0], y, ssd_out_b, j)
            zs, dts = _norm_matmul(h[1], norm_mix[i], ssd_in_b, j, 0, n_zx, extra=prm["w_dt"])
            zs3 = zs.reshape(bs, n_tok, -1)
            pad_rows = ((0, 0), (0, lc - n_tok), (0, 0))
            zs_pad = jnp.pad(zs3, pad_rows).reshape(bs * lc, -1)
            dts_pad = jnp.pad(dts.reshape(bs, n_tok, -1), pad_rows).reshape(bs * lc, -1)
            tail = jnp.pad(state_conv[j].astype(F32), ((0, 0), (CONV_TAIL - (kw - 1), 0), (0, 0)))
            y, h_last = _ssd_core(zs_pad, dts_pad, bs, 1, n_tok, dims, prm,
                                  init=(tail, state_ssm[j].astype(F32)))
            xbc = zs3[:, :, d_inner:]
            conv_out[1].append(jnp.concatenate([state_conv[j].astype(F32), xbc], axis=1)[:, n_tok:])
            ssm_out[1].append(h_last)
            y = y.reshape(bs, lc, d_inner)[:, :n_tok].reshape(bs * n_tok, d_inner)
            h[1] = _matmul_residual(h[1], y, ssd_out_b, j)
        else:
            lambda_init = 0.8 - 0.6 * math.exp(-0.3 * i)
            qk_gain = attn_qk_norm[j]
            q1, q2 = _norm_matmul(h[0], norm_mix[i], attn_qkv_b, j, 0, att_w, "q", qk_gain[0], hd)
            kp_stack, kb = _norm_matmul(h[0], norm_mix[i], attn_qkv_b, j, att_w, att_w, "k", qk_gain[1], hd,
                                        stack=(j, n_attn, kp_stack))
            vp_stack, vt = _norm_matmul(h[0], norm_mix[i], attn_qkv_b, j, 2 * att_w, att_w, "vt",
                                        stack=(j, n_attn, vp_stack), tm=_pick(seq, (512, 256, 128)))
            o = _attn_prompt(q1, q2, kb, vt, bp, seq, n_heads, attn_lambda[j], attn_sub_norm[j], lambda_init)
            h[0] = _matmul_residual(h[0], o, attn_out_b, j)
            q1, q2 = _norm_matmul(h[1], norm_mix[i], attn_qkv_b, j, 0, att_w, "q", qk_gain[0], hd)
            kf, _ = _norm_matmul(h[1], norm_mix[i], attn_qkv_b, j, att_w, att_w, "k", qk_gain[1], hd)
            vf = _norm_matmul(h[1], norm_mix[i], attn_qkv_b, j, 2 * att_w, att_w, "v")
            o = _attn_sample(q1, q2, kf, vf, cache_k, cache_v, j, page_table, bs, n_tok, n_heads,
                             attn_lambda[j], attn_sub_norm[j], lambda_init)
            ks_l.append(kf.reshape(bs, n_tok, n_heads, hw))
            vs_l.append(vf.reshape(bs, n_tok, n_heads, hw))
            h[1] = _matmul_residual(h[1], o, attn_out_b, j)
        h = [_ffn(x, norm_ffb[i], ffb_in_b, ffb_out_b, i) for x in h]
        h = [_ple(x, ple_in[gi][i], norm_ple[i], ple_proj_b, ple_gate_b, i) for gi, x in enumerate(h)]

    return (h[0].reshape(bp, seq, d_model), h[1].reshape(bs, n_tok, d_model),
            kp_stack.reshape(n_attn, bp, seq, n_heads, hw), vp_stack.reshape(n_attn, bp, seq, n_heads, hw),
            jnp.stack(ks_l), jnp.stack(vs_l),
            jnp.stack(conv_out[0]), jnp.stack(ssm_out[0]), jnp.stack(conv_out[1]), jnp.stack(ssm_out[1]))
```

```python
import functools
import math

import jax
import jax.numpy as jnp
from jax import lax
from jax.experimental import pallas as pl
from jax.experimental.pallas import tpu as pltpu

F32 = jnp.float32
BF16 = jnp.bfloat16
NORM_EPS = 1e-6
LOG2E = 1.4426950408889634
SSD_CHUNK = 128
LANES = 128
CONV_TAIL = 8
DEC_PAGES_PER_STEP = 4
QKV_COL_TILE = 1024
VMEM_LIMIT_BYTES = 56 * 1024 * 1024
ROW_TILES = (512, 256, 128, 64, 32, 16, 8)


def _cparams(*sem):
    return pltpu.CompilerParams(dimension_semantics=sem, vmem_limit_bytes=VMEM_LIMIT_BYTES)


def _pick(n, prefs):
    for t in prefs:
        if n % t == 0:
            return t
    return n


def _silu(x):
    return x * jax.nn.sigmoid(x)


def _rms(x, g):
    return x * lax.rsqrt(jnp.mean(x * x, axis=-1, keepdims=True) + NORM_EPS) * g


def _dot(a, b):
    return jnp.dot(a, b, preferred_element_type=F32)


def _dot_nt(a, b):
    return lax.dot_general(a, b, (((1,), (1,)), ((), ())), preferred_element_type=F32)


def _split3(x):
    hi = x.astype(BF16)
    r = x - hi.astype(F32)
    mid = r.astype(BF16)
    lo = (r - mid.astype(F32)).astype(BF16)
    return hi, mid, lo


def _rows_reduce(x, op):
    rows = x.shape[0]
    slabs = 8 if rows % 64 == 0 else 1
    part = op(x.reshape(slabs, rows // slabs, x.shape[1]), axis=0)
    return op(part, axis=0, keepdims=True)


def _lambda_full(lam_ref, lambda_init):
    lf = lam_ref[...]
    return (jnp.exp(jnp.sum(lf[0:1] * lf[1:2], axis=-1, keepdims=True))
            - jnp.exp(jnp.sum(lf[2:3] * lf[3:4], axis=-1, keepdims=True)) + lambda_init)


def _ffn_kernel(h_ref, g_ref, wa_ref, wb_ref, wo_ref, o_ref, xn_ref, acc_ref):
    f = pl.program_id(1)

    @pl.when(f == 0)
    def _():
        xn_ref[...] = _rms(h_ref[...], g_ref[...]).astype(BF16)
        acc_ref[...] = jnp.zeros_like(acc_ref)

    xn = xn_ref[...]
    a = _dot(xn, wa_ref[...])
    b = _dot(xn, wb_ref[...])
    acc_ref[...] += _dot((_silu(a) * b).astype(BF16), wo_ref[...])

    @pl.when(f == pl.num_programs(1) - 1)
    def _():
        o_ref[...] = h_ref[...] + 0.5 * acc_ref[...]


def _ffn(h, g, w_in, w_out, layer):
    m, d = h.shape
    dff = w_out.shape[1]
    tm = _pick(m, ROW_TILES)
    tf = _pick(dff, (512, 256, 128))
    nf = dff // tf
    return pl.pallas_call(
        _ffn_kernel,
        grid=(m // tm, nf),
        in_specs=[
            pl.BlockSpec((tm, d), lambda i, f: (i, 0)),
            pl.BlockSpec((1, d), lambda i, f: (0, 0)),
            pl.BlockSpec((None, d, tf), lambda i, f: (layer, 0, f)),
            pl.BlockSpec((None, d, tf), lambda i, f: (layer, 0, f + nf)),
            pl.BlockSpec((None, tf, d), lambda i, f: (layer, f, 0)),
        ],
        out_specs=pl.BlockSpec((tm, d), lambda i, f: (i, 0)),
        out_shape=jax.ShapeDtypeStruct((m, d), F32),
        scratch_shapes=[pltpu.VMEM((tm, d), BF16), pltpu.VMEM((tm, d), F32)],
        compiler_params=_cparams("parallel", "arbitrary"),
        name="ffn",
    )(h, g.reshape(1, d), w_in, w_in, w_out)


def _ple_kernel(h_ref, p_ref, g_ref, wp_ref, wg_ref, o_ref):
    h = h_ref[...]
    xn = _rms(h, g_ref[...]).astype(BF16)
    gate = jax.nn.sigmoid(_dot(xn, wg_ref[...]))
    proj = _dot(p_ref[...].astype(BF16), wp_ref[...])
    o_ref[...] = h + proj * gate


def _ple(h, p, g, w_proj, w_gate, layer):
    m, d = h.shape
    pd = p.shape[1]
    tm = _pick(m, (256, 128, 64, 32, 16, 8))
    return pl.pallas_call(
        _ple_kernel,
        grid=(m // tm,),
        in_specs=[
            pl.BlockSpec((tm, d), lambda i: (i, 0)),
            pl.BlockSpec((tm, pd), lambda i: (i, 0)),
            pl.BlockSpec((1, d), lambda i: (0, 0)),
            pl.BlockSpec((None, pd, d), lambda i: (layer, 0, 0)),
            pl.BlockSpec((None, d, d), lambda i: (layer, 0, 0)),
        ],
        out_specs=pl.BlockSpec((tm, d), lambda i: (i, 0)),
        out_shape=jax.ShapeDtypeStruct((m, d), F32),
        compiler_params=_cparams("parallel"),
        name="ple",
    )(h, p, g.reshape(1, d), w_proj, w_gate)


def _inproj_kernel(h_ref, g_ref, w_ref, wx_ref, y_ref, yx_ref, xn_ref):
    @pl.when(pl.program_id(1) == 0)
    def _():
        xn_ref[...] = _rms(h_ref[...], g_ref[...]).astype(BF16)
        yx_ref[...] = _dot(xn_ref[...], wx_ref[...])

    y_ref[...] = _dot(xn_ref[...], w_ref[...])


def _ssd_inproj(h, g, w, layer, n, w_dt):
    m, d = h.shape
    nx = w_dt.shape[1]
    tm = _pick(m, ROW_TILES)
    tn = _pick(n, (1024, 512, 256, 128))
    return pl.pallas_call(
        _inproj_kernel,
        grid=(m // tm, n // tn),
        in_specs=[
            pl.BlockSpec((tm, d), lambda i, j: (i, 0)),
            pl.BlockSpec((1, d), lambda i, j: (0, 0)),
            pl.BlockSpec((None, d, tn), lambda i, j: (layer, 0, j)),
            pl.BlockSpec((d, nx), lambda i, j: (0, 0)),
        ],
        out_specs=[pl.BlockSpec((tm, tn), lambda i, j: (i, j)),
                   pl.BlockSpec((tm, nx), lambda i, j: (i, 0))],
        out_shape=[jax.ShapeDtypeStruct((m, n), F32), jax.ShapeDtypeStruct((m, nx), F32)],
        scratch_shapes=[pltpu.VMEM((tm, d), BF16)],
        compiler_params=_cparams("parallel", "arbitrary"),
        name="ssd_inproj",
    )(h, g.reshape(1, d), w, w_dt)


def _seg_meansq(y, seg_ref, hd):
    y2 = y * y
    hi = y2.astype(BF16)
    lo = (y2 - hi.astype(F32)).astype(BF16)
    seg = seg_ref[...]
    return (_dot(hi, seg) + _dot(lo, seg)) * (1.0 / hd)


def _qkv_kernel(*refs, hd, nsec, with_vt, n_prev):
    h_ref, g_ref, w_ref, gq_ref, gk_ref, seg_ref = refs[:6]
    outs = refs[6 + n_prev:-1]
    q1_ref, q2_ref, kf_ref, kb_ref, vf_ref = outs[:5]
    xn_ref = refs[-1]
    j = pl.program_id(1)

    @pl.when(j == 0)
    def _():
        xn_ref[...] = _rms(h_ref[...], g_ref[...]).astype(BF16)

    y = _dot(xn_ref[...], w_ref[...])
    cw = seg_ref.shape[0]

    def head_norm(gain_ref):
        parts = []
        for c in range(y.shape[1] // cw):
            yc = y[:, c * cw:(c + 1) * cw]
            parts.append(yc * lax.rsqrt(_seg_meansq(yc, seg_ref, hd) + NORM_EPS))
        yn = parts[0] if len(parts) == 1 else jnp.concatenate(parts, axis=1)
        return yn * gain_ref[...]

    @pl.when(j < nsec)
    def _():
        yn = head_norm(gq_ref) * (LOG2E / math.sqrt(hd))
        lane = lax.broadcasted_iota(jnp.int32, yn.shape, 1)
        first = (lane % (2 * hd)) < hd
        q1_ref[...] = jnp.where(first, yn, 0.0).astype(BF16)
        q2_ref[...] = jnp.where(first, 0.0, yn).astype(BF16)

    @pl.when((j >= nsec) & (j < 2 * nsec))
    def _():
        yn = head_norm(gk_ref)
        kf_ref[...] = yn
        kb_ref[...] = yn.astype(BF16)

    @pl.when(j >= 2 * nsec)
    def _():
        vf_ref[...] = y
        if with_vt:
            outs[5][...] = y.T.astype(BF16)


def _qkv(h, g, w, layer, n, qk_gain, hd, tm=None, with_vt=False, stack=None):
    m, d = h.shape
    tm = tm or _pick(m, ROW_TILES)
    tn = _pick(n, (QKV_COL_TILE, 512, 256))
    cw = min(tn, 256)
    nsec = n // tn

    def sec(j, s):
        return jnp.clip(j - s * nsec, 0, nsec - 1)

    lane = jnp.arange(cw)
    seg = (lane[:, None] // hd == lane[None, :] // hd).astype(BF16)
    gains = [jnp.tile(qk_gain[s].reshape(1, 2 * hd).astype(F32), (1, tn // (2 * hd))) for s in range(2)]
    in_specs = [
        pl.BlockSpec((tm, d), lambda i, j: (i, 0)),
        pl.BlockSpec((1, d), lambda i, j: (0, 0)),
        pl.BlockSpec((None, d, tn), lambda i, j: (layer, 0, j)),
        pl.BlockSpec((1, tn), lambda i, j: (0, 0)),
        pl.BlockSpec((1, tn), lambda i, j: (0, 0)),
        pl.BlockSpec((cw, cw), lambda i, j: (0, 0)),
    ]
    args = [h, g.reshape(1, d), w, gains[0], gains[1], seg]

    def tile_spec(s):
        return pl.BlockSpec((tm, tn), lambda i, j: (i, sec(j, s)))

    aliases = {}
    if stack is not None:
        slot, slots, prev = stack

        def f32_spec(s):
            return pl.BlockSpec((None, tm, tn), lambda i, j: (slot, i, sec(j, s)))

        f32_shape = jax.ShapeDtypeStruct((slots, m, n), F32)
        if prev is not None:
            aliases = {len(args): 2, len(args) + 1: 4}
            in_specs += [pl.BlockSpec(memory_space=pl.ANY)] * 2
            args += list(prev)
    else:
        f32_spec, f32_shape = tile_spec, jax.ShapeDtypeStruct((m, n), F32)
    out_specs = [tile_spec(0), tile_spec(0), f32_spec(1), tile_spec(1), f32_spec(2)]
    out_shape = [jax.ShapeDtypeStruct((m, n), BF16)] * 2 + [f32_shape, jax.ShapeDtypeStruct((m, n), BF16),
                                                            f32_shape]
    if with_vt:
        out_specs.append(pl.BlockSpec((None, tn, tm), lambda i, j: (i, sec(j, 2), 0)))
        out_shape.append(jax.ShapeDtypeStruct((m // tm, n, tm), BF16))
    return pl.pallas_call(
        functools.partial(_qkv_kernel, hd=hd, nsec=nsec, with_vt=with_vt, n_prev=len(aliases)),
        grid=(m // tm, 3 * nsec),
        in_specs=in_specs,
        out_specs=out_specs,
        out_shape=out_shape,
        scratch_shapes=[pltpu.VMEM((tm, d), BF16)],
        input_output_aliases=aliases,
        compiler_params=_cparams("parallel", "arbitrary"),
        name="qkv_proj",
    )(*args)


def _mm_res_kernel(h_ref, x_ref, w_ref, o_ref):
    o_ref[...] = h_ref[...] + _dot(x_ref[...].astype(BF16), w_ref[...])


def _matmul_residual(h, x, w, layer):
    m, n = h.shape
    k = x.shape[1]
    tm = _pick(m, ROW_TILES)
    tn = _pick(n, (512, 256, 128))
    return pl.pallas_call(
        _mm_res_kernel,
        grid=(m // tm, n // tn),
        in_specs=[
            pl.BlockSpec((tm, tn), lambda i, j: (i, j)),
            pl.BlockSpec((tm, k), lambda i, j: (i, 0)),
            pl.BlockSpec((None, k, tn), lambda i, j: (layer, 0, j)),
        ],
        out_specs=pl.BlockSpec((tm, tn), lambda i, j: (i, j)),
        out_shape=jax.ShapeDtypeStruct((m, n), F32),
        compiler_params=_cparams("parallel", "arbitrary"),
        name="matmul_residual",
    )(h, x, w)


def _ssd_kernel(*refs, lc, valid, has_init, hpg, pdim):
    (z_ref, x_ref, b_ref, c_ref, dt_ref) = refs[:5]
    pos = 5
    if has_init:
        tx_ref, tb_ref, tc_ref, h0_ref = refs[pos:pos + 4]
        pos += 4
    (cwx_ref, cwb_ref, cwc_ref, cbx_ref, cbb_ref, cbc_ref,
     dtb_ref, alog_ref, dsk_ref, ng_ref) = refs[pos:pos + 10]
    pos += 10
    y_ref, hout_ref = refs[pos:pos + 2]
    ht_ref, ex_ref, eb_ref, ec_ref = refs[pos + 2:]

    c = pl.program_id(2)
    w = hpg * pdim
    nst = ht_ref.shape[0]
    t0 = CONV_TAIL

    @pl.when(c == 0)
    def _():
        if has_init:
            ex_ref[0:t0, :] = tx_ref[0]
            eb_ref[0:t0, :] = tb_ref[0]
            ec_ref[0:t0, :] = tc_ref[0]
            ht_ref[...] = h0_ref[0].reshape(w, nst).T
        else:
            ex_ref[0:t0, :] = jnp.zeros((t0, w), F32)
            eb_ref[0:t0, :] = jnp.zeros((t0, nst), F32)
            ec_ref[0:t0, :] = jnp.zeros((t0, nst), F32)
            ht_ref[...] = jnp.zeros_like(ht_ref)

    def conv_act(e_ref, raw_ref, cw_ref, cb_ref):
        e_ref[t0:t0 + lc, :] = raw_ref[...]
        kw = cw_ref.shape[0]
        acc = cb_ref[...] + cw_ref[kw - 1:kw, :] * e_ref[t0:t0 + lc, :]
        for k in range(kw - 1):
            off = t0 - (kw - 1) + k
            acc = acc + cw_ref[k:k + 1, :] * e_ref[off:off + lc, :]
        e_ref[0:t0, :] = e_ref[lc:lc + t0, :]
        return _silu(acc)

    xc = conv_act(ex_ref, x_ref, cwx_ref, cbx_ref)
    bc = conv_act(eb_ref, b_ref, cwb_ref, cbb_ref)
    cc = conv_act(ec_ref, c_ref, cwc_ref, cbc_ref)

    row = lax.broadcasted_iota(jnp.int32, (lc, lc), 0)
    col = lax.broadcasted_iota(jnp.int32, (lc, lc), 1)
    causal = row >= col
    lane = lax.broadcasted_iota(jnp.int32, (lc, LANES), 1)

    dtr = dt_ref[...] + dtb_ref[...]
    dt = jnp.maximum(dtr, 0.0) + jnp.log1p(jnp.exp(-jnp.abs(dtr)))
    if valid < lc:
        dt = jnp.where(lax.broadcasted_iota(jnp.int32, (lc, LANES), 0) < valid, dt, 0.0)
    da = dt * (-jnp.exp(alog_ref[...]))
    tri = causal.astype(BF16)
    d_hi, d_mid, d_lo = _split3(da)
    cs = _dot(tri, d_hi) + _dot(tri, d_mid) + _dot(tri, d_lo)
    cs_t = cs.T
    ecs = jnp.exp(cs)
    coef_s = dt * jnp.exp(cs[lc - 1:lc, :] - cs)

    def head_col(arr, j):
        return jnp.sum(jnp.where(lane == j, arr, 0.0), axis=-1, keepdims=True)

    first_half = lane < pdim

    def expand(arr):
        blks = [jnp.where(first_half, head_col(arr, 2 * q), head_col(arr, 2 * q + 1))
                for q in range(hpg // 2)]
        return blks[0] if len(blks) == 1 else jnp.concatenate(blks, axis=1)

    dt_x = expand(dt)
    coef_x = expand(coef_s)
    ecs_x = expand(ecs)

    bcb = bc.astype(BF16)
    ccb = cc.astype(BF16)
    cb = _dot_nt(ccb, bcb)
    xd = xc * dt_x

    def decay_mat(j):
        seg = head_col(cs, j) - cs_t[j:j + 1, :]
        return (cb * jnp.exp(jnp.where(causal, seg, -jnp.inf))).astype(BF16)

    y_blks = []
    for q in range(hpg // 2):
        mcat = jnp.concatenate([decay_mat(2 * q), decay_mat(2 * q + 1)], axis=1)
        xp = xd[:, q * LANES:(q + 1) * LANES]
        rhs = jnp.concatenate([jnp.where(first_half, xp, 0.0), jnp.where(first_half, 0.0, xp)],
                              axis=0).astype(BF16)
        y_blks.append(_dot(mcat, rhs))
    y = y_blks[0] if len(y_blks) == 1 else jnp.concatenate(y_blks, axis=1)

    ht_in = ht_ref[...]
    y = y + _dot(ccb, ht_in.astype(BF16)) * ecs_x
    upd = _dot(bc.T.astype(BF16), (xc * coef_x).astype(BF16))
    ht_ref[...] = ht_in * ecs_x[lc - 1:lc, :] + upd

    y = y + xc * dsk_ref[...]
    y = y * _silu(z_ref[...])
    y = y * lax.rsqrt(jnp.mean(y * y, axis=-1, keepdims=True) + NORM_EPS) * ng_ref[...]
    y_ref[...] = y.astype(BF16)

    @pl.when(c == pl.num_programs(2) - 1)
    def _():
        hout_ref[0] = ht_ref[...].T.reshape(hpg, pdim, nst)


def _ssd_core(zx, dtg, bsz, n_chunks, valid, dims, prm, init=None):
    d_inner, nst, groups, heads = dims
    hpg = heads // groups
    pdim = d_inner // heads
    w = hpg * pdim
    lc = SSD_CHUNK
    assert pdim * 2 == LANES and hpg % 2 == 0 and nst == LANES
    rows = bsz * n_chunks * lc
    xo, bo, co = d_inner // w, 2 * d_inner // nst, (2 * d_inner + groups * nst) // nst
    cbo, cco = d_inner // nst, (d_inner + groups * nst) // nst

    def rowblk(b, g, c):
        return b * n_chunks + c

    in_specs = [
        pl.BlockSpec((lc, w), lambda b, g, c: (rowblk(b, g, c), g)),
        pl.BlockSpec((lc, w), lambda b, g, c: (rowblk(b, g, c), xo + g)),
        pl.BlockSpec((lc, nst), lambda b, g, c: (rowblk(b, g, c), bo + g)),
        pl.BlockSpec((lc, nst), lambda b, g, c: (rowblk(b, g, c), co + g)),
        pl.BlockSpec((lc, LANES), lambda b, g, c: (rowblk(b, g, c), g)),
    ]
    args = [zx, zx, zx, zx, dtg]
    if init is not None:
        tail, h0 = init
        in_specs += [
            pl.BlockSpec((1, CONV_TAIL, w), lambda b, g, c: (b, 0, g)),
            pl.BlockSpec((1, CONV_TAIL, nst), lambda b, g, c: (b, 0, cbo + g)),
            pl.BlockSpec((1, CONV_TAIL, nst), lambda b, g, c: (b, 0, cco + g)),
            pl.BlockSpec((1, hpg, pdim, nst), lambda b, g, c: (b, g, 0, 0)),
        ]
        args += [tail, tail, tail, h0]
    kw = prm["conv_w"].shape[0]
    in_specs += [
        pl.BlockSpec((kw, w), lambda b, g, c: (0, g)),
        pl.BlockSpec((kw, nst), lambda b, g, c: (0, cbo + g)),
        pl.BlockSpec((kw, nst), lambda b, g, c: (0, cco + g)),
        pl.BlockSpec((1, w), lambda b, g, c: (0, g)),
        pl.BlockSpec((1, nst), lambda b, g, c: (0, cbo + g)),
        pl.BlockSpec((1, nst), lambda b, g, c: (0, cco + g)),
        pl.BlockSpec((1, LANES), lambda b, g, c: (0, g)),
        pl.BlockSpec((1, LANES), lambda b, g, c: (0, g)),
        pl.BlockSpec((1, w), lambda b, g, c: (0, g)),
        pl.BlockSpec((1, w), lambda b, g, c: (0, g)),
    ]
    args += [prm["conv_w"], prm["conv_w"], prm["conv_w"], prm["conv_b"], prm["conv_b"], prm["conv_b"],
             prm["dt_bias_g"], prm["a_log_g"], prm["d_x"], prm["norm_g"]]
    y, h_last = pl.pallas_call(
        functools.partial(_ssd_kernel, lc=lc, valid=valid, has_init=init is not None, hpg=hpg, pdim=pdim),
        grid=(bsz, groups, n_chunks),
        in_specs=in_specs,
        out_specs=[
            pl.BlockSpec((lc, w), lambda b, g, c: (rowblk(b, g, c), g)),
            pl.BlockSpec((1, hpg, pdim, nst), lambda b, g, c: (b, g, 0, 0)),
        ],
        out_shape=[jax.ShapeDtypeStruct((rows, d_inner), BF16),
                   jax.ShapeDtypeStruct((bsz, heads, pdim, nst), F32)],
        scratch_shapes=[pltpu.VMEM((nst, w), F32),
                        pltpu.VMEM((lc + CONV_TAIL, w), F32),
                        pltpu.VMEM((lc + CONV_TAIL, nst), F32),
                        pltpu.VMEM((lc + CONV_TAIL, nst), F32)],
        compiler_params=_cparams("arbitrary", "arbitrary", "arbitrary"),
        name="ssd_scan",
    )(*args)
    return y, h_last


def _attn_kernel(q1_ref, q2_ref, k_ref, vt_ref, slope_ref, lam_ref, sn_ref, o_ref,
                 m1_ref, l1_ref, a1_ref, m2_ref, l2_ref, a2_ref, *, tq, lambda_init):
    qi = pl.program_id(2)
    hw = k_ref.shape[1]
    slope2 = slope_ref[0][:, 0:1] * LOG2E
    key_row = lax.broadcasted_iota(jnp.int32, (tq, LANES), 0).astype(F32)
    streams = ((q1_ref, m1_ref, l1_ref, a1_ref), (q2_ref, m2_ref, l2_ref, a2_ref))
    for _, m_ref, l_ref, a_ref in streams:
        m_ref[...] = jnp.full_like(m_ref, -jnp.inf)
        l_ref[...] = jnp.zeros_like(l_ref)
        a_ref[...] = jnp.zeros_like(a_ref)

    def tile(ki, masked):
        start = pl.multiple_of(ki * tq, tq)
        k = k_ref[pl.ds(start, tq), :]
        vt = vt_ref[ki]
        kb = slope2 * (key_row + start.astype(F32))
        kb = jnp.concatenate([kb] * (tq // LANES), axis=1)
        if masked:
            keep = (lax.broadcasted_iota(jnp.int32, (tq, tq), 0)
                    <= lax.broadcasted_iota(jnp.int32, (tq, tq), 1))
        scores = [_dot_nt(k, q_ref[...]) + kb for q_ref, _, _, _ in streams]
        if masked:
            scores = [jnp.where(keep, s, -jnp.inf) for s in scores]
        m_olds = [m_ref[...] for _, m_ref, _, _ in streams]
        m_news = [jnp.maximum(m_old, _rows_reduce(s, jnp.max)) for m_old, s in zip(m_olds, scores)]
        probs = [jnp.exp2(s - m_new) for s, m_new in zip(scores, m_news)]
        pvs = [_dot(vt, p.astype(BF16)) for p in probs]
        for (_, m_ref, l_ref, a_ref), m_old, m_new, p, pv in zip(streams, m_olds, m_news, probs, pvs):
            alpha = jnp.exp2(m_old - m_new)
            l_ref[...] = alpha * l_ref[...] + _rows_reduce(p, jnp.sum)
            a_ref[...] = alpha * a_ref[...] + pv
            m_ref[...] = m_new

    def body(ki, carry):
        tile(ki, False)
        return carry

    lax.fori_loop(0, qi, body, 0)
    tile(qi, True)

    lam = _lambda_full(lam_ref, lambda_init)
    ot = a1_ref[...] / l1_ref[...] - lam * (a2_ref[...] / l2_ref[...])
    ot = ot * lax.rsqrt(jnp.mean(ot * ot, axis=0, keepdims=True) + NORM_EPS)
    o_ref[...] = (ot.T * (sn_ref[...] * (1.0 - lambda_init))).astype(BF16)


def _alibi_slopes(n_heads):
    return jnp.exp2(-8.0 * jnp.arange(1, n_heads + 1, dtype=F32) / n_heads)


def _attn_prompt(q1, q2, kb, vt, bsz, seq, n_heads, lam_p, sub_norm, lambda_init):
    m, width = q1.shape
    hw = width // n_heads
    tq = vt.shape[2]
    nq = seq // tq
    assert seq % tq == 0 and tq % LANES == 0
    slopes = jnp.broadcast_to(_alibi_slopes(n_heads)[:, None, None], (n_heads, 1, LANES))
    return pl.pallas_call(
        functools.partial(_attn_kernel, tq=tq, lambda_init=lambda_init),
        grid=(bsz, n_heads, nq),
        in_specs=[
            pl.BlockSpec((tq, hw), lambda b, h, i: (b * nq + i, h)),
            pl.BlockSpec((tq, hw), lambda b, h, i: (b * nq + i, h)),
            pl.BlockSpec((seq, hw), lambda b, h, i: (b, h)),
            pl.BlockSpec((nq, hw, tq), lambda b, h, i: (b, h, 0)),
            pl.BlockSpec((1, 1, LANES), lambda b, h, i: (h, 0, 0)),
            pl.BlockSpec(lam_p.shape, lambda b, h, i: (0, 0)),
            pl.BlockSpec((1, hw), lambda b, h, i: (0, 0)),
        ],
        out_specs=pl.BlockSpec((tq, hw), lambda b, h, i: (b * nq + i, h)),
        out_shape=jax.ShapeDtypeStruct((m, width), BF16),
        scratch_shapes=[pltpu.VMEM((1, tq), F32), pltpu.VMEM((1, tq), F32), pltpu.VMEM((hw, tq), F32),
                        pltpu.VMEM((1, tq), F32), pltpu.VMEM((1, tq), F32), pltpu.VMEM((hw, tq), F32)],
        compiler_params=_cparams("parallel", "parallel", "arbitrary"),
        name="attn_prompt",
    )(q1, q2, kb, vt, slopes, lam_p.astype(F32), sub_norm.reshape(1, hw).astype(F32))


def _attn_dec_kernel(*refs, pps, n_steps, n_tok, n_heads, lambda_init):
    q_ref = refs[1]
    kp_refs, vp_refs = refs[2:2 + pps], refs[2 + pps:2 + 2 * pps]
    (kn_ref, vn_ref, base_ref, basen_ref, slc_ref, lam_ref, sn_ref,
     o_ref, m_ref, l_ref, acc_ref) = refs[2 + 2 * pps:]
    st = pl.program_id(1)
    rph = 2 * n_tok

    @pl.when(st == 0)
    def _():
        m_ref[...] = jnp.full_like(m_ref, -jnp.inf)
        l_ref[...] = jnp.zeros_like(l_ref)
        acc_ref[...] = jnp.zeros_like(acc_ref)

    q = q_ref[0]

    def flat(x3):
        return x3.reshape(x3.shape[0] * x3.shape[1], x3.shape[2]).astype(BF16)

    def update(pages, bias_ref, shifts):
        m_old = m_ref[...]
        m_new = m_old
        scores = []
        for (k_ref3, _), shift in zip(pages, shifts):
            s = _dot_nt(q, flat(k_ref3())) + bias_ref[...]
            scores.append(s)
            m_new = jnp.maximum(m_new, jnp.max(s, axis=-1, keepdims=True) + shift)
        alpha = jnp.exp2(m_old - m_new)
        l_new = alpha * l_ref[...]
        acc = alpha * acc_ref[...]
        for s, (_, v_ref3), shift in zip(scores, pages, shifts):
            p = jnp.exp2(s - (m_new - shift))
            l_new = l_new + jnp.sum(p, axis=-1, keepdims=True)
            acc = acc + _dot(p.astype(BF16), flat(v_ref3()))
        l_ref[...] = l_new
        acc_ref[...] = acc
        m_ref[...] = m_new

    @pl.when(st < n_steps)
    def _():
        first = (st * pps).astype(F32)
        update([(lambda r=kp_refs[i]: r[0, 0], lambda r=vp_refs[i]: r[0, 0]) for i in range(pps)],
               base_ref, [slc_ref[...] * (first + float(i)) for i in range(pps)])

    @pl.when(st == n_steps)
    def _():
        update([(lambda: kn_ref[0], lambda: vn_ref[0])], basen_ref, [0.0])
        lam = _lambda_full(lam_ref, lambda_init)
        on = acc_ref[...] / l_ref[...]
        hw = on.shape[1]
        for h in range(n_heads):
            o = on[h * rph:h * rph + n_tok, :] - lam * on[h * rph + n_tok:(h + 1) * rph, :]
            o = o * lax.rsqrt(jnp.mean(o * o, axis=-1, keepdims=True) + NORM_EPS)
            o_ref[0, :, h * hw:(h + 1) * hw] = (o * (sn_ref[...] * (1.0 - lambda_init))).astype(o_ref.dtype)


def _attn_sample(q1, q2, k_new, v_new, cache_k, cache_v, layer, page_table, bsz, n_tok, n_heads,
                 lam_p, sub_norm, lambda_init):
    page, hw = cache_k.shape[2], cache_k.shape[4]
    n_pages = page_table.shape[1]
    past = n_pages * page
    rph = 2 * n_tok
    nrow = rph * n_heads
    ncol = page * n_heads
    assert n_tok <= page and rph % 8 == 0 and n_heads % 8 == 0
    q12 = jnp.stack([q1, q2]).reshape(2, bsz, n_tok, n_heads, hw)
    qs = jnp.transpose(q12, (1, 3, 0, 2, 4)).reshape(bsz, nrow, hw)
    pad = ((0, 0), (0, page - n_tok), (0, 0), (0, 0))
    kn = jnp.pad(k_new.reshape(bsz, n_tok, n_heads, hw), pad)
    vn = jnp.pad(v_new.reshape(bsz, n_tok, n_heads, hw), pad)
    r = jnp.arange(nrow)
    c = jnp.arange(ncol)
    row_head, row_tok = r // rph, (r % n_tok).astype(F32)
    col_key, col_head = (c // n_heads).astype(F32), c % n_heads
    slope2 = (_alibi_slopes(n_heads) * LOG2E)[row_head]
    same_head = row_head[:, None] == col_head[None, :]
    dist_page = past + row_tok[:, None] - col_key[None, :]
    base = jnp.where(same_head, -slope2[:, None] * dist_page, -jnp.inf)
    dist_new = row_tok[:, None] - col_key[None, :]
    ok_new = same_head & (col_key[None, :] < n_tok) & (dist_new >= 0)
    base_new = jnp.where(ok_new, -slope2[:, None] * dist_new, -jnp.inf)
    slope_page = (slope2 * page).reshape(nrow, 1)

    pps = _pick(n_pages, (DEC_PAGES_PER_STEP, 2, 1))
    n_steps = n_pages // pps

    def page_idx(i, b, p, pt):
        return (layer, pt[b, jnp.minimum(p, n_steps - 1) * pps + i], 0, 0, 0)

    def const2(b, p, pt):
        return (0, 0)

    page_specs = [pl.BlockSpec((1, 1, page, n_heads, hw), functools.partial(page_idx, i)) for i in range(pps)]
    grid_spec = pltpu.PrefetchScalarGridSpec(
        num_scalar_prefetch=1,
        grid=(bsz, n_steps + 1),
        in_specs=[
            pl.BlockSpec((1, nrow, hw), lambda b, p, pt: (b, 0, 0)),
            *page_specs,
            *page_specs,
            pl.BlockSpec((1, page, n_heads, hw), lambda b, p, pt: (b, 0, 0, 0)),
            pl.BlockSpec((1, page, n_heads, hw), lambda b, p, pt: (b, 0, 0, 0)),
            pl.BlockSpec((nrow, ncol), const2),
            pl.BlockSpec((nrow, ncol), const2),
            pl.BlockSpec((nrow, 1), const2),
            pl.BlockSpec(lam_p.shape, const2),
            pl.BlockSpec((1, hw), const2),
        ],
        out_specs=pl.BlockSpec((1, n_tok, n_heads * hw), lambda b, p, pt: (b, 0, 0)),
        scratch_shapes=[pltpu.VMEM((nrow, 1), F32), pltpu.VMEM((nrow, 1), F32), pltpu.VMEM((nrow, hw), F32)],
    )
    out = pl.pallas_call(
        functools.partial(_attn_dec_kernel, pps=pps, n_steps=n_steps, n_tok=n_tok, n_heads=n_heads,
                          lambda_init=lambda_init),
        grid_spec=grid_spec,
        out_shape=jax.ShapeDtypeStruct((bsz, n_tok, n_heads * hw), BF16),
        compiler_params=_cparams("arbitrary", "arbitrary"),
        name="attn_sample",
    )(page_table, qs, *([cache_k] * pps), *([cache_v] * pps), kn, vn, base, base_new, slope_page,
      lam_p.astype(F32), sub_norm.reshape(1, hw).astype(F32))
    return out.reshape(bsz * n_tok, n_heads * hw)


def _ssd_params(j, dims, ssd_in, conv_w, conv_b, dt_bias, a_log, d_skip, norm_g):
    d_inner, nst, groups, heads = dims
    hpg = heads // groups
    pdim = d_inner // heads
    conv_dim = conv_w.shape[2]
    d_model = ssd_in.shape[1]

    def per_group(v):
        return jnp.pad(v.astype(F32).reshape(groups, hpg), ((0, 0), (0, LANES - hpg))).reshape(1, groups * LANES)

    w_dt = ssd_in[j][:, d_inner + conv_dim:].reshape(d_model, groups, hpg)
    w_dt = jnp.pad(w_dt, ((0, 0), (0, 0), (0, LANES - hpg))).reshape(d_model, groups * LANES)
    return dict(
        w_dt=w_dt.astype(BF16),
        conv_w=conv_w[j].astype(F32),
        conv_b=conv_b[j].reshape(1, conv_dim).astype(F32),
        dt_bias_g=per_group(dt_bias[j]),
        a_log_g=per_group(a_log[j]),
        d_x=jnp.repeat(d_skip[j].astype(F32), pdim).reshape(1, d_inner),
        norm_g=norm_g[j].reshape(1, d_inner).astype(F32),
    )


def kernel(x_prompt, x_sample, cache_k, cache_v, state_conv, state_ssm, page_table, p_prompt, p_sample, norm_ffa, ffa_in, ffa_out, norm_mix, norm_ffb, ffb_in, ffb_out, norm_ple, ple_proj, ple_gate, ssd_in, ssd_conv_w, ssd_conv_b, ssd_dt_bias, ssd_a_log, ssd_d, ssd_norm, ssd_out, attn_qkv, attn_qk_norm, attn_lambda, attn_sub_norm, attn_out):
    bp, seq, d_model = x_prompt.shape
    bs, n_tok, _ = x_sample.shape
    depth = norm_ffa.shape[0]
    n_mixers = 2
    n_attn = depth // n_mixers
    heads = ssd_dt_bias.shape[1]
    d_inner = ssd_out.shape[1]
    nst = state_ssm.shape[-1]
    conv_dim = ssd_conv_w.shape[2]
    kw = ssd_conv_w.shape[1]
    groups = (conv_dim - d_inner) // (2 * nst)
    dims = (d_inner, nst, groups, heads)
    n_heads = cache_k.shape[3]
    hw = cache_k.shape[4]
    hd = hw // 2
    att_w = n_heads * hw
    lc = SSD_CHUNK
    assert seq % lc == 0 and n_tok <= lc and n_tok >= kw - 1 and kw - 1 <= CONV_TAIL

    ffa_in_b, ffa_out_b = ffa_in.astype(BF16), ffa_out.astype(BF16)
    ffb_in_b, ffb_out_b = ffb_in.astype(BF16), ffb_out.astype(BF16)
    ple_proj_b, ple_gate_b = ple_proj.astype(BF16), ple_gate.astype(BF16)
    ssd_in_b, ssd_out_b = ssd_in.astype(BF16), ssd_out.astype(BF16)
    attn_qkv_b, attn_out_b = attn_qkv.astype(BF16), attn_out.astype(BF16)

    h = [x_prompt.reshape(bp * seq, d_model), x_sample.reshape(bs * n_tok, d_model)]
    ple_in = [p_prompt.reshape(depth, bp * seq, -1), p_sample.reshape(depth, bs * n_tok, -1)]
    kp_stack = vp_stack = kv_prev = None
    ks_l, vs_l = [], []
    conv_out, ssm_out = [[], []], [[], []]

    for i in range(depth):
        j = i // n_mixers
        h = [_ffn(x, norm_ffa[i], ffa_in_b, ffa_out_b, i) for x in h]
        if i % n_mixers == 0:
            prm = _ssd_params(j, dims, ssd_in, ssd_conv_w, ssd_conv_b, ssd_dt_bias, ssd_a_log, ssd_d, ssd_norm)
            n_zx = d_inner + conv_dim
            zx, dtg = _ssd_inproj(h[0], norm_mix[i], ssd_in_b, j, n_zx, prm["w_dt"])
            y, h_last = _ssd_core(zx, dtg, bp, seq // lc, lc, dims, prm)
            conv_out[0].append(zx.reshape(bp, seq, -1)[:, seq - (kw - 1):, d_inner:])
            ssm_out[0].append(h_last)
            h[0] = _matmul_residual(h[0], y, ssd_out_b, j)
            zs, dts = _ssd_inproj(h[1], norm_mix[i], ssd_in_b, j, n_zx, prm["w_dt"])
            zs3 = zs.reshape(bs, n_tok, -1)
            pad_rows = ((0, 0), (0, lc - n_tok), (0, 0))
            zs_pad = jnp.pad(zs3, pad_rows).reshape(bs * lc, -1)
            dts_pad = jnp.pad(dts.reshape(bs, n_tok, -1), pad_rows).reshape(bs * lc, -1)
            tail = jnp.pad(state_conv[j].astype(F32), ((0, 0), (CONV_TAIL - (kw - 1), 0), (0, 0)))
            y, h_last = _ssd_core(zs_pad, dts_pad, bs, 1, n_tok, dims, prm,
                                  init=(tail, state_ssm[j].astype(F32)))
            xbc = zs3[:, :, d_inner:]
            conv_out[1].append(jnp.concatenate([state_conv[j].astype(F32), xbc], axis=1)[:, n_tok:])
            ssm_out[1].append(h_last)
            y = y.reshape(bs, lc, d_inner)[:, :n_tok].reshape(bs * n_tok, d_inner)
            h[1] = _matmul_residual(h[1], y, ssd_out_b, j)
        else:
            lambda_init = 0.8 - 0.6 * math.exp(-0.3 * i)
            qk_gain = attn_qk_norm[j]
            q1, q2, kp_stack, kb, vp_stack, vt = _qkv(
                h[0], norm_mix[i], attn_qkv_b, j, att_w, qk_gain, hd, tm=_pick(seq, (512, 256, 128)),
                with_vt=True, stack=(j, n_attn, kv_prev))
            kv_prev = (kp_stack, vp_stack)
            o = _attn_prompt(q1, q2, kb, vt, bp, seq, n_heads, attn_lambda[j], attn_sub_norm[j], lambda_init)
            h[0] = _matmul_residual(h[0], o, attn_out_b, j)
            q1, q2, kf, _, vf = _qkv(h[1], norm_mix[i], attn_qkv_b, j, att_w, qk_gain, hd)
            o = _attn_sample(q1, q2, kf, vf, cache_k, cache_v, j, page_table, bs, n_tok, n_heads,
                             attn_lambda[j], attn_sub_norm[j], lambda_init)
            ks_l.append(kf.reshape(bs, n_tok, n_heads, hw))
            vs_l.append(vf.reshape(bs, n_tok, n_heads, hw))
            h[1] = _matmul_residual(h[1], o, attn_out_b, j)
        h = [_ffn(x, norm_ffb[i], ffb_in_b, ffb_out_b, i) for x in h]
        h = [_ple(x, ple_in[gi][i], norm_ple[i], ple_proj_b, ple_gate_b, i) for gi, x in enumerate(h)]

    return (h[0].reshape(bp, seq, d_model), h[1].reshape(bs, n_tok, d_model),
            kp_stack.reshape(n_attn, bp, seq, n_heads, hw), vp_stack.reshape(n_attn, bp, seq, n_heads, hw),
            jnp.stack(ks_l), jnp.stack(vs_l),
            jnp.stack(conv_out[0]), jnp.stack(ssm_out[0]), jnp.stack(conv_out[1]), jnp.stack(ssm_out[1]))
```

```python
import functools
import math

import jax
import jax.numpy as jnp
from jax import lax
from jax.experimental import pallas as pl
from jax.experimental.pallas import tpu as pltpu

F32 = jnp.float32
BF16 = jnp.bfloat16
NORM_EPS = 1e-6
LOG2E = 1.4426950408889634
SSD_CHUNK = 128
LANES = 128
BF16_ROWS = 16
CONV_TAIL = 8
DEC_PAGES_PER_STEP = 4
SSD_GROUPS_PER_STEP = 2
QKV_COL_TILE = 1024
VMEM_LIMIT_BYTES = 56 * 1024 * 1024
ROW_TILES = (512, 256, 128, 64, 32, 16, 8)


def _cparams(*sem):
    return pltpu.CompilerParams(dimension_semantics=sem, vmem_limit_bytes=VMEM_LIMIT_BYTES)


def _pick(n, prefs):
    for t in prefs:
        if n % t == 0:
            return t
    return n


def _silu(x):
    return x * jax.nn.sigmoid(x)


def _rms(x, g):
    return x * lax.rsqrt(jnp.mean(x * x, axis=-1, keepdims=True) + NORM_EPS) * g


def _dot(a, b):
    return jnp.dot(a, b, preferred_element_type=F32)


def _dot_nt(a, b):
    return lax.dot_general(a, b, (((1,), (1,)), ((), ())), preferred_element_type=F32)


def _split3(x):
    hi = x.astype(BF16)
    r = x - hi.astype(F32)
    mid = r.astype(BF16)
    lo = (r - mid.astype(F32)).astype(BF16)
    return hi, mid, lo


def _rows_reduce(x, op):
    rows = x.shape[0]
    slabs = 8 if rows % 64 == 0 else 1
    part = op(x.reshape(slabs, rows // slabs, x.shape[1]), axis=0)
    return op(part, axis=0, keepdims=True)


def _lambda_full(lam_ref, lambda_init):
    lf = lam_ref[...]
    return (jnp.exp(jnp.sum(lf[0:1] * lf[1:2], axis=-1, keepdims=True))
            - jnp.exp(jnp.sum(lf[2:3] * lf[3:4], axis=-1, keepdims=True)) + lambda_init)


def _ffn_kernel(h_ref, g_ref, wa_ref, wb_ref, wo_ref, o_ref, *rest, emit):
    xn_ref, acc_ref = rest[-2:]
    f = pl.program_id(1)

    @pl.when(f == 0)
    def _():
        xn_ref[...] = _rms(h_ref[...], g_ref[...]).astype(BF16)
        acc_ref[...] = jnp.zeros_like(acc_ref)

    wa, wb, wo = wa_ref[...].astype(BF16), wb_ref[...].astype(BF16), wo_ref[...].astype(BF16)
    if emit:
        rest[0][...], rest[1][...], rest[2][...] = wa, wb, wo
    xn = xn_ref[...]
    a = _dot(xn, wa)
    b = _dot(xn, wb)
    acc_ref[...] += _dot((_silu(a) * b).astype(BF16), wo)

    @pl.when(f == pl.num_programs(1) - 1)
    def _():
        o_ref[...] = h_ref[...] + 0.5 * acc_ref[...]


def _ffn(h, g, wa, wb, wo, layer=None):
    m, d = h.shape
    emit = layer is not None
    dff = wo.shape[-2]
    tm = _pick(m, ROW_TILES)
    tf = _pick(dff, (512, 256, 128))
    nf = dff // tf
    if emit:
        assert m == tm
        w_specs = [pl.BlockSpec((None, d, tf), lambda i, f: (layer, 0, f)),
                   pl.BlockSpec((None, d, tf), lambda i, f: (layer, 0, f + nf)),
                   pl.BlockSpec((None, tf, d), lambda i, f: (layer, f, 0))]
    else:
        w_specs = [pl.BlockSpec((d, tf), lambda i, f: (0, f)),
                   pl.BlockSpec((d, tf), lambda i, f: (0, f)),
                   pl.BlockSpec((tf, d), lambda i, f: (f, 0))]
    out_specs = [pl.BlockSpec((tm, d), lambda i, f: (i, 0))]
    out_shape = [jax.ShapeDtypeStruct((m, d), F32)]
    if emit:
        out_specs += [pl.BlockSpec((d, tf), lambda i, f: (0, f)), pl.BlockSpec((d, tf), lambda i, f: (0, f)),
                      pl.BlockSpec((tf, d), lambda i, f: (f, 0))]
        out_shape += [jax.ShapeDtypeStruct((d, dff), BF16), jax.ShapeDtypeStruct((d, dff), BF16),
                      jax.ShapeDtypeStruct((dff, d), BF16)]
    out = pl.pallas_call(
        functools.partial(_ffn_kernel, emit=emit),
        grid=(m // tm, nf),
        in_specs=[pl.BlockSpec((tm, d), lambda i, f: (i, 0)), pl.BlockSpec((1, d), lambda i, f: (0, 0))] + w_specs,
        out_specs=out_specs,
        out_shape=out_shape,
        scratch_shapes=[pltpu.VMEM((tm, d), BF16), pltpu.VMEM((tm, d), F32)],
        compiler_params=_cparams("parallel", "arbitrary"),
        name="ffn_cast" if emit else "ffn",
    )(h, g.reshape(1, d), wa, wb, wo)
    return out if emit else out[0]


def _ple_kernel(h_ref, p_ref, g_ref, wp_ref, wg_ref, o_ref):
    h = h_ref[...]
    xn = _rms(h, g_ref[...]).astype(BF16)
    gate = jax.nn.sigmoid(_dot(xn, wg_ref[...]))
    proj = _dot(p_ref[...].astype(BF16), wp_ref[...])
    o_ref[...] = h + proj * gate


def _ple(h, p, g, w_proj, w_gate, layer):
    m, d = h.shape
    pd = p.shape[1]
    tm = _pick(m, (256, 128, 64, 32, 16, 8))
    return pl.pallas_call(
        _ple_kernel,
        grid=(m // tm,),
        in_specs=[
            pl.BlockSpec((tm, d), lambda i: (i, 0)),
            pl.BlockSpec((tm, pd), lambda i: (i, 0)),
            pl.BlockSpec((1, d), lambda i: (0, 0)),
            pl.BlockSpec((None, pd, d), lambda i: (layer, 0, 0)),
            pl.BlockSpec((None, d, d), lambda i: (layer, 0, 0)),
        ],
        out_specs=pl.BlockSpec((tm, d), lambda i: (i, 0)),
        out_shape=jax.ShapeDtypeStruct((m, d), F32),
        compiler_params=_cparams("parallel"),
        name="ple",
    )(h, p, g.reshape(1, d), w_proj, w_gate)


def _inproj_kernel(h_ref, g_ref, w_ref, wx_ref, y_ref, yx_ref, xn_ref):
    @pl.when(pl.program_id(1) == 0)
    def _():
        xn_ref[...] = _rms(h_ref[...], g_ref[...]).astype(BF16)
        yx_ref[...] = _dot(xn_ref[...], wx_ref[...])

    y_ref[...] = _dot(xn_ref[...], w_ref[...])


def _ssd_inproj(h, g, w, layer, n, w_dt):
    m, d = h.shape
    nx = w_dt.shape[1]
    tm = _pick(m, ROW_TILES)
    tn = _pick(n, (1024, 512, 256, 128))
    return pl.pallas_call(
        _inproj_kernel,
        grid=(m // tm, n // tn),
        in_specs=[
            pl.BlockSpec((tm, d), lambda i, j: (i, 0)),
            pl.BlockSpec((1, d), lambda i, j: (0, 0)),
            pl.BlockSpec((None, d, tn), lambda i, j: (layer, 0, j)),
            pl.BlockSpec((d, nx), lambda i, j: (0, 0)),
        ],
        out_specs=[pl.BlockSpec((tm, tn), lambda i, j: (i, j)),
                   pl.BlockSpec((tm, nx), lambda i, j: (i, 0))],
        out_shape=[jax.ShapeDtypeStruct((m, n), F32), jax.ShapeDtypeStruct((m, nx), F32)],
        scratch_shapes=[pltpu.VMEM((tm, d), BF16)],
        compiler_params=_cparams("parallel", "arbitrary"),
        name="ssd_inproj",
    )(h, g.reshape(1, d), w, w_dt)


def _seg_meansq(y, seg_ref, hd):
    y2 = y * y
    hi = y2.astype(BF16)
    lo = (y2 - hi.astype(F32)).astype(BF16)
    seg = seg_ref[...]
    return (_dot(hi, seg) + _dot(lo, seg)) * (1.0 / hd)


def _qkv_kernel(*refs, hd, nsec, with_vt, n_prev):
    h_ref, g_ref, w_ref, gq_ref, gk_ref, seg_ref = refs[:6]
    outs = refs[6 + n_prev:-1]
    q1_ref, q2_ref, kf_ref, kb_ref, vf_ref = outs[:5]
    xn_ref = refs[-1]
    j = pl.program_id(1)

    @pl.when(j == 0)
    def _():
        xn_ref[...] = _rms(h_ref[...], g_ref[...]).astype(BF16)

    y = _dot(xn_ref[...], w_ref[...])
    cw = seg_ref.shape[0]

    def head_norm(gain_ref):
        parts = []
        for c in range(y.shape[1] // cw):
            yc = y[:, c * cw:(c + 1) * cw]
            parts.append(yc * lax.rsqrt(_seg_meansq(yc, seg_ref, hd) + NORM_EPS))
        yn = parts[0] if len(parts) == 1 else jnp.concatenate(parts, axis=1)
        return yn * gain_ref[...]

    @pl.when(j < nsec)
    def _():
        yn = head_norm(gq_ref) * (LOG2E / math.sqrt(hd))
        lane = lax.broadcasted_iota(jnp.int32, yn.shape, 1)
        first = (lane % (2 * hd)) < hd
        q1_ref[...] = jnp.where(first, yn, 0.0).astype(BF16)
        q2_ref[...] = jnp.where(first, 0.0, yn).astype(BF16)

    @pl.when((j >= nsec) & (j < 2 * nsec))
    def _():
        yn = head_norm(gk_ref)
        kf_ref[...] = yn
        kb_ref[...] = yn.astype(BF16)

    @pl.when(j >= 2 * nsec)
    def _():
        vf_ref[...] = y
        if with_vt:
            outs[5][...] = y.T.astype(BF16)


def _qkv(h, g, w, layer, n, qk_gain, hd, tm=None, with_vt=False, stack=None):
    m, d = h.shape
    tm = tm or _pick(m, ROW_TILES)
    tn = _pick(n, (QKV_COL_TILE, 512, 256))
    cw = min(tn, 256)
    nsec = n // tn

    def sec(j, s):
        return jnp.clip(j - s * nsec, 0, nsec - 1)

    lane = jnp.arange(cw)
    seg = (lane[:, None] // hd == lane[None, :] // hd).astype(BF16)
    gains = [jnp.tile(qk_gain[s].reshape(1, 2 * hd).astype(F32), (1, tn // (2 * hd))) for s in range(2)]
    in_specs = [
        pl.BlockSpec((tm, d), lambda i, j: (i, 0)),
        pl.BlockSpec((1, d), lambda i, j: (0, 0)),
        pl.BlockSpec((None, d, tn), lambda i, j: (layer, 0, j)),
        pl.BlockSpec((1, tn), lambda i, j: (0, 0)),
        pl.BlockSpec((1, tn), lambda i, j: (0, 0)),
        pl.BlockSpec((cw, cw), lambda i, j: (0, 0)),
    ]
    args = [h, g.reshape(1, d), w, gains[0], gains[1], seg]

    def tile_spec(s):
        return pl.BlockSpec((tm, tn), lambda i, j: (i, sec(j, s)))

    aliases = {}
    if stack is not None:
        slot, slots, prev = stack

        def f32_spec(s):
            return pl.BlockSpec((None, tm, tn), lambda i, j: (slot, i, sec(j, s)))

        f32_shape = jax.ShapeDtypeStruct((slots, m, n), F32)
        if prev is not None:
            aliases = {len(args): 2, len(args) + 1: 4}
            in_specs += [pl.BlockSpec(memory_space=pl.ANY)] * 2
            args += list(prev)
    else:
        f32_spec, f32_shape = tile_spec, jax.ShapeDtypeStruct((m, n), F32)
    out_specs = [tile_spec(0), tile_spec(0), f32_spec(1), tile_spec(1), f32_spec(2)]
    out_shape = [jax.ShapeDtypeStruct((m, n), BF16)] * 2 + [f32_shape, jax.ShapeDtypeStruct((m, n), BF16),
                                                            f32_shape]
    if with_vt:
        out_specs.append(pl.BlockSpec((None, tn, tm), lambda i, j: (i, sec(j, 2), 0)))
        out_shape.append(jax.ShapeDtypeStruct((m // tm, n, tm), BF16))
    return pl.pallas_call(
        functools.partial(_qkv_kernel, hd=hd, nsec=nsec, with_vt=with_vt, n_prev=len(aliases)),
        grid=(m // tm, 3 * nsec),
        in_specs=in_specs,
        out_specs=out_specs,
        out_shape=out_shape,
        scratch_shapes=[pltpu.VMEM((tm, d), BF16)],
        input_output_aliases=aliases,
        compiler_params=_cparams("parallel", "arbitrary"),
        name="qkv_proj",
    )(*args)


def _mm_res_kernel(h_ref, x_ref, w_ref, o_ref):
    o_ref[...] = h_ref[...] + _dot(x_ref[...].astype(BF16), w_ref[...])


def _matmul_residual(h, x, w, layer):
    m, n = h.shape
    k = x.shape[1]
    tm = _pick(m, ROW_TILES)
    tn = _pick(n, (512, 256, 128))
    return pl.pallas_call(
        _mm_res_kernel,
        grid=(m // tm, n // tn),
        in_specs=[
            pl.BlockSpec((tm, tn), lambda i, j: (i, j)),
            pl.BlockSpec((tm, k), lambda i, j: (i, 0)),
            pl.BlockSpec((None, k, tn), lambda i, j: (layer, 0, j)),
        ],
        out_specs=pl.BlockSpec((tm, tn), lambda i, j: (i, j)),
        out_shape=jax.ShapeDtypeStruct((m, n), F32),
        compiler_params=_cparams("parallel", "arbitrary"),
        name="matmul_residual",
    )(h, x, w)


def _ssd_kernel(*refs, lc, valid, has_init, hpg, pdim, gps):
    (z_ref, x_ref, b_ref, c_ref, dt_ref) = refs[:5]
    pos = 5
    if has_init:
        tx_ref, tb_ref, tc_ref, h0_ref = refs[pos:pos + 4]
        pos += 4
    (cwx_ref, cwb_ref, cwc_ref, cbx_ref, cbb_ref, cbc_ref,
     dtb_ref, alog_ref, dsk_ref, ng_ref, hx_ref) = refs[pos:pos + 11]
    pos += 11
    y_ref, hout_ref = refs[pos:pos + 2]
    ht_ref, ex_ref, eb_ref, ec_ref = refs[pos + 2:]

    c = pl.program_id(2)
    w = hpg * pdim
    nst = ht_ref.shape[0]
    t0 = CONV_TAIL

    @pl.when(c == 0)
    def _():
        if has_init:
            ex_ref[0:t0, :] = tx_ref[0]
            eb_ref[0:t0, :] = tb_ref[0]
            ec_ref[0:t0, :] = tc_ref[0]
            ht_ref[...] = h0_ref[0].reshape(gps * w, nst).T
        else:
            ex_ref[0:t0, :] = jnp.zeros((t0, gps * w), F32)
            eb_ref[0:t0, :] = jnp.zeros((t0, gps * nst), F32)
            ec_ref[0:t0, :] = jnp.zeros((t0, gps * nst), F32)
            ht_ref[...] = jnp.zeros_like(ht_ref)

    def conv_act(e_ref, raw_ref, cw_ref, cb_ref, lanes):
        e_ref[t0:t0 + lc, lanes] = raw_ref[:, lanes]
        kw = cw_ref.shape[0]
        acc = cb_ref[:, lanes] + cw_ref[kw - 1:kw, lanes] * e_ref[t0:t0 + lc, lanes]
        for k in range(kw - 1):
            off = t0 - (kw - 1) + k
            acc = acc + cw_ref[k:k + 1, lanes] * e_ref[off:off + lc, lanes]
        e_ref[0:t0, lanes] = e_ref[lc:lc + t0, lanes]
        return _silu(acc)

    row = lax.broadcasted_iota(jnp.int32, (lc, lc), 0)
    col = lax.broadcasted_iota(jnp.int32, (lc, lc), 1)
    causal = row >= col
    tri = causal.astype(BF16)
    lane = lax.broadcasted_iota(jnp.int32, (lc, LANES), 1)
    first_half = lane < pdim

    def one_group(gi):
        wl = slice(gi * w, (gi + 1) * w)
        nl = slice(gi * nst, (gi + 1) * nst)
        hl = slice(gi * LANES, (gi + 1) * LANES)
        xc = conv_act(ex_ref, x_ref, cwx_ref, cbx_ref, wl)
        bc = conv_act(eb_ref, b_ref, cwb_ref, cbb_ref, nl)
        cc = conv_act(ec_ref, c_ref, cwc_ref, cbc_ref, nl)

        dtr = dt_ref[:, hl] + dtb_ref[:, hl]
        dt = jnp.maximum(dtr, 0.0) + jnp.log1p(jnp.exp(-jnp.abs(dtr)))
        if valid < lc:
            dt = jnp.where(lax.broadcasted_iota(jnp.int32, (lc, LANES), 0) < valid, dt, 0.0)
        da = dt * (-jnp.exp(alog_ref[:, hl]))
        d_hi, d_mid, d_lo = _split3(da)
        cs = _dot(tri, d_hi) + _dot(tri, d_mid) + _dot(tri, d_lo)
        cs_t = cs.T
        ecs = jnp.exp(cs)
        coef_s = dt * jnp.exp(cs[lc - 1:lc, :] - cs)

        e_hi, e_mid, e_lo = _split3(ecs)
        ex = _dot(jnp.concatenate([dt.astype(BF16), coef_s.astype(BF16), e_hi, e_mid, e_lo], axis=0),
                  hx_ref[...])
        dt_x, coef_x = ex[0:lc], ex[lc:2 * lc]
        ecs_x = ex[2 * lc:3 * lc] + ex[3 * lc:4 * lc] + ex[4 * lc:5 * lc]

        bcb = bc.astype(BF16)
        ccb = cc.astype(BF16)
        cb = _dot_nt(ccb, bcb)
        xd = xc * dt_x

        def decay_mat(j):
            col_j = jnp.sum(jnp.where(lane == j, cs, 0.0), axis=-1, keepdims=True)
            return (cb * jnp.exp(jnp.where(causal, col_j - cs_t[j:j + 1, :], -jnp.inf))).astype(BF16)

        y_blks = []
        for q in range(hpg // 2):
            mcat = jnp.concatenate([decay_mat(2 * q), decay_mat(2 * q + 1)], axis=1)
            xp = xd[:, q * LANES:(q + 1) * LANES]
            rhs = jnp.concatenate([jnp.where(first_half, xp, 0.0), jnp.where(first_half, 0.0, xp)],
                                  axis=0).astype(BF16)
            y_blks.append(_dot(mcat, rhs))
        y = y_blks[0] if len(y_blks) == 1 else jnp.concatenate(y_blks, axis=1)

        ht_in = ht_ref[:, wl]
        y = y + _dot(ccb, ht_in.astype(BF16)) * ecs_x
        upd = _dot(bc.T.astype(BF16), (xc * coef_x).astype(BF16))
        ht_ref[:, wl] = ht_in * ecs_x[lc - 1:lc, :] + upd

        y = y + xc * dsk_ref[:, wl]
        y = y * _silu(z_ref[:, wl])
        y = y * lax.rsqrt(jnp.mean(y * y, axis=-1, keepdims=True) + NORM_EPS) * ng_ref[:, wl]
        y_ref[:, wl] = y.astype(BF16)

    for gi in range(gps):
        one_group(gi)

    @pl.when(c == pl.num_programs(2) - 1)
    def _():
        hout_ref[0] = ht_ref[...].T.reshape(gps * hpg, pdim, nst)


def _ssd_core(zx, dtg, bsz, n_chunks, valid, dims, prm, init=None):
    d_inner, nst, groups, heads = dims
    hpg = heads // groups
    pdim = d_inner // heads
    w = hpg * pdim
    lc = SSD_CHUNK
    assert pdim * 2 == LANES and hpg % 2 == 0 and nst == LANES
    gps = _pick(groups, (SSD_GROUPS_PER_STEP, 1))
    gw, gn, gl = gps * w, gps * nst, gps * LANES
    rows = bsz * n_chunks * lc
    xo, bo, co = d_inner // gw, 2 * d_inner // gn, (2 * d_inner + groups * nst) // gn
    cbo, cco = d_inner // gn, (d_inner + groups * nst) // gn
    head_expand = (jnp.arange(LANES)[:, None] == jnp.arange(w)[None, :] // pdim).astype(BF16)

    def rowblk(b, g, c):
        return b * n_chunks + c

    in_specs = [
        pl.BlockSpec((lc, gw), lambda b, g, c: (rowblk(b, g, c), g)),
        pl.BlockSpec((lc, gw), lambda b, g, c: (rowblk(b, g, c), xo + g)),
        pl.BlockSpec((lc, gn), lambda b, g, c: (rowblk(b, g, c), bo + g)),
        pl.BlockSpec((lc, gn), lambda b, g, c: (rowblk(b, g, c), co + g)),
        pl.BlockSpec((lc, gl), lambda b, g, c: (rowblk(b, g, c), g)),
    ]
    args = [zx, zx, zx, zx, dtg]
    if init is not None:
        tail, h0, h0_layer = init
        in_specs += [
            pl.BlockSpec((1, CONV_TAIL, gw), lambda b, g, c: (b, 0, g)),
            pl.BlockSpec((1, CONV_TAIL, gn), lambda b, g, c: (b, 0, cbo + g)),
            pl.BlockSpec((1, CONV_TAIL, gn), lambda b, g, c: (b, 0, cco + g)),
            pl.BlockSpec((None, 1, gps * hpg, pdim, nst), lambda b, g, c: (h0_layer, b, g, 0, 0)),
        ]
        args += [tail, tail, tail, h0]
    kw = prm["conv_w"].shape[0]
    in_specs += [
        pl.BlockSpec((kw, gw), lambda b, g, c: (0, g)),
        pl.BlockSpec((kw, gn), lambda b, g, c: (0, cbo + g)),
        pl.BlockSpec((kw, gn), lambda b, g, c: (0, cco + g)),
        pl.BlockSpec((1, gw), lambda b, g, c: (0, g)),
        pl.BlockSpec((1, gn), lambda b, g, c: (0, cbo + g)),
        pl.BlockSpec((1, gn), lambda b, g, c: (0, cco + g)),
        pl.BlockSpec((1, gl), lambda b, g, c: (0, g)),
        pl.BlockSpec((1, gl), lambda b, g, c: (0, g)),
        pl.BlockSpec((1, gw), lambda b, g, c: (0, g)),
        pl.BlockSpec((1, gw), lambda b, g, c: (0, g)),
        pl.BlockSpec((LANES, w), lambda b, g, c: (0, 0)),
    ]
    args += [prm["conv_w"], prm["conv_w"], prm["conv_w"], prm["conv_b"], prm["conv_b"], prm["conv_b"],
             prm["dt_bias_g"], prm["a_log_g"], prm["d_x"], prm["norm_g"], head_expand]
    y, h_last = pl.pallas_call(
        functools.partial(_ssd_kernel, lc=lc, valid=valid, has_init=init is not None, hpg=hpg, pdim=pdim,
                          gps=gps),
        grid=(bsz, groups // gps, n_chunks),
        in_specs=in_specs,
        out_specs=[
            pl.BlockSpec((lc, gw), lambda b, g, c: (rowblk(b, g, c), g)),
            pl.BlockSpec((1, gps * hpg, pdim, nst), lambda b, g, c: (b, g, 0, 0)),
        ],
        out_shape=[jax.ShapeDtypeStruct((rows, d_inner), BF16),
                   jax.ShapeDtypeStruct((bsz, heads, pdim, nst), F32)],
        scratch_shapes=[pltpu.VMEM((nst, gw), F32),
                        pltpu.VMEM((lc + CONV_TAIL, gw), F32),
                        pltpu.VMEM((lc + CONV_TAIL, gn), F32),
                        pltpu.VMEM((lc + CONV_TAIL, gn), F32)],
        compiler_params=_cparams("arbitrary", "arbitrary", "arbitrary"),
        name="ssd_scan",
    )(*args)
    return y, h_last


def _attn_kernel(q1_ref, q2_ref, k_ref, vt_ref, slope_ref, lam_ref, sn_ref, o_ref,
                 m1_ref, a1_ref, m2_ref, a2_ref, *, tq, n_q, lambda_init):
    qi = pl.program_id(2)
    hw = k_ref.shape[1]
    slope2 = slope_ref[0][:, 0:1] * LOG2E
    key_row = lax.broadcasted_iota(jnp.int32, (tq, LANES), 0).astype(F32)
    streams = ((q1_ref, m1_ref, a1_ref), (q2_ref, m2_ref, a2_ref))
    for _, m_ref, a_ref in streams:
        m_ref[...] = jnp.full_like(m_ref, -jnp.inf)
        a_ref[...] = jnp.zeros_like(a_ref)
    ones_rows = jnp.ones((a1_ref.shape[0] - hw, tq), BF16)

    def tile(ki, masked):
        k = k_ref[ki * tq:(ki + 1) * tq, :]
        vt = jnp.concatenate([vt_ref[ki], ones_rows], axis=0)
        kb = slope2 * (key_row + float(ki * tq))
        kb = jnp.concatenate([kb] * (tq // LANES), axis=1)
        scores = [_dot_nt(k, q_ref[...]) + kb for q_ref, _, _ in streams]
        if masked:
            keep = (lax.broadcasted_iota(jnp.int32, (tq, tq), 0)
                    <= lax.broadcasted_iota(jnp.int32, (tq, tq), 1))
            scores = [jnp.where(keep, s, -jnp.inf) for s in scores]
        m_olds = [m_ref[...] for _, m_ref, _ in streams]
        m_news = [jnp.maximum(m_old, _rows_reduce(s, jnp.max)) for m_old, s in zip(m_olds, scores)]
        probs = [jnp.exp2((s - m_new).astype(BF16)) for s, m_new in zip(scores, m_news)]
        pvs = [_dot(vt, p) for p in probs]
        for (_, m_ref, a_ref), m_old, m_new, pv in zip(streams, m_olds, m_news, pvs):
            a_ref[...] = jnp.exp2(m_old - m_new) * a_ref[...] + pv
            m_ref[...] = m_new

    for v in range(n_q):
        @pl.when(qi == v)
        def _(v=v):
            for ki in range(v):
                tile(ki, False)
            tile(v, True)

    lam = _lambda_full(lam_ref, lambda_init)
    ot = (a1_ref[0:hw, :] / a1_ref[hw:hw + 1, :]
          - lam * (a2_ref[0:hw, :] / a2_ref[hw:hw + 1, :]))
    ot = ot * lax.rsqrt(jnp.mean(ot * ot, axis=0, keepdims=True) + NORM_EPS)
    o_ref[...] = (ot.T * (sn_ref[...] * (1.0 - lambda_init))).astype(BF16)


def _alibi_slopes(n_heads):
    return jnp.exp2(-8.0 * jnp.arange(1, n_heads + 1, dtype=F32) / n_heads)


def _attn_prompt(q1, q2, kb, vt, bsz, seq, n_heads, lam_p, sub_norm, lambda_init):
    m, width = q1.shape
    hw = width // n_heads
    tq = vt.shape[2]
    nq = seq // tq
    assert seq % tq == 0 and tq % LANES == 0
    slopes = jnp.broadcast_to(_alibi_slopes(n_heads)[:, None, None], (n_heads, 1, LANES))
    return pl.pallas_call(
        functools.partial(_attn_kernel, tq=tq, n_q=nq, lambda_init=lambda_init),
        grid=(bsz, n_heads, nq),
        in_specs=[
            pl.BlockSpec((tq, hw), lambda b, h, i: (b * nq + i, h)),
            pl.BlockSpec((tq, hw), lambda b, h, i: (b * nq + i, h)),
            pl.BlockSpec((seq, hw), lambda b, h, i: (b, h)),
            pl.BlockSpec((nq, hw, tq), lambda b, h, i: (b, h, 0)),
            pl.BlockSpec((1, 1, LANES), lambda b, h, i: (h, 0, 0)),
            pl.BlockSpec(lam_p.shape, lambda b, h, i: (0, 0)),
            pl.BlockSpec((1, hw), lambda b, h, i: (0, 0)),
        ],
        out_specs=pl.BlockSpec((tq, hw), lambda b, h, i: (b * nq + i, h)),
        out_shape=jax.ShapeDtypeStruct((m, width), BF16),
        scratch_shapes=[pltpu.VMEM((1, tq), F32), pltpu.VMEM((hw + BF16_ROWS, tq), F32),
                        pltpu.VMEM((1, tq), F32), pltpu.VMEM((hw + BF16_ROWS, tq), F32)],
        compiler_params=_cparams("parallel", "parallel", "arbitrary"),
        name="attn_prompt",
    )(q1, q2, kb, vt, slopes, lam_p.astype(F32), sub_norm.reshape(1, hw).astype(F32))


def _attn_dec_kernel(*refs, pps, n_steps, n_tok, n_heads, lambda_init):
    q_ref = refs[1]
    kp_refs, vp_refs = refs[2:2 + pps], refs[2 + pps:2 + 2 * pps]
    (kn_ref, vn_ref, base_ref, basen_ref, slc_ref, lam_ref, sn_ref,
     o_ref, m_ref, l_ref, acc_ref) = refs[2 + 2 * pps:]
    st = pl.program_id(1)
    rph = 2 * n_tok

    @pl.when(st == 0)
    def _():
        m_ref[...] = jnp.full_like(m_ref, -jnp.inf)
        l_ref[...] = jnp.zeros_like(l_ref)
        acc_ref[...] = jnp.zeros_like(acc_ref)

    q = q_ref[0]

    def flat(x3):
        return x3.reshape(x3.shape[0] * x3.shape[1], x3.shape[2]).astype(BF16)

    def update(pages, bias_ref, shifts):
        m_old = m_ref[...]
        m_new = m_old
        scores = []
        for (k_ref3, _), shift in zip(pages, shifts):
            s = _dot_nt(q, flat(k_ref3())) + bias_ref[...]
            scores.append(s)
            m_new = jnp.maximum(m_new, jnp.max(s, axis=-1, keepdims=True) + shift)
        alpha = jnp.exp2(m_old - m_new)
        l_new = alpha * l_ref[...]
        acc = alpha * acc_ref[...]
        for s, (_, v_ref3), shift in zip(scores, pages, shifts):
            p = jnp.exp2(s - (m_new - shift))
            l_new = l_new + jnp.sum(p, axis=-1, keepdims=True)
            acc = acc + _dot(p.astype(BF16), flat(v_ref3()))
        l_ref[...] = l_new
        acc_ref[...] = acc
        m_ref[...] = m_new

    @pl.when(st < n_steps)
    def _():
        first = (st * pps).astype(F32)
        update([(lambda r=kp_refs[i]: r[0, 0], lambda r=vp_refs[i]: r[0, 0]) for i in range(pps)],
               base_ref, [slc_ref[...] * (first + float(i)) for i in range(pps)])

    @pl.when(st == n_steps)
    def _():
        update([(lambda: kn_ref[0], lambda: vn_ref[0])], basen_ref, [0.0])
        lam = _lambda_full(lam_ref, lambda_init)
        on = acc_ref[...] / l_ref[...]
        hw = on.shape[1]
        for h in range(n_heads):
            o = on[h * rph:h * rph + n_tok, :] - lam * on[h * rph + n_tok:(h + 1) * rph, :]
            o = o * lax.rsqrt(jnp.mean(o * o, axis=-1, keepdims=True) + NORM_EPS)
            o_ref[0, :, h * hw:(h + 1) * hw] = (o * (sn_ref[...] * (1.0 - lambda_init))).astype(o_ref.dtype)


def _attn_sample(q1, q2, k_new, v_new, cache_k, cache_v, layer, page_table, bsz, n_tok, n_heads,
                 lam_p, sub_norm, lambda_init):
    page, hw = cache_k.shape[2], cache_k.shape[4]
    n_pages = page_table.shape[1]
    past = n_pages * page
    rph = 2 * n_tok
    nrow = rph * n_heads
    ncol = page * n_heads
    assert n_tok <= page and rph % 8 == 0 and n_heads % 8 == 0
    q12 = jnp.stack([q1, q2]).reshape(2, bsz, n_tok, n_heads, hw)
    qs = jnp.transpose(q12, (1, 3, 0, 2, 4)).reshape(bsz, nrow, hw)
    pad = ((0, 0), (0, page - n_tok), (0, 0), (0, 0))
    kn = jnp.pad(k_new.reshape(bsz, n_tok, n_heads, hw), pad)
    vn = jnp.pad(v_new.reshape(bsz, n_tok, n_heads, hw), pad)
    r = jnp.arange(nrow)
    c = jnp.arange(ncol)
    row_head, row_tok = r // rph, (r % n_tok).astype(F32)
    col_key, col_head = (c // n_heads).astype(F32), c % n_heads
    slope2 = (_alibi_slopes(n_heads) * LOG2E)[row_head]
    same_head = row_head[:, None] == col_head[None, :]
    dist_page = past + row_tok[:, None] - col_key[None, :]
    base = jnp.where(same_head, -slope2[:, None] * dist_page, -jnp.inf)
    dist_new = row_tok[:, None] - col_key[None, :]
    ok_new = same_head & (col_key[None, :] < n_tok) & (dist_new >= 0)
    base_new = jnp.where(ok_new, -slope2[:, None] * dist_new, -jnp.inf)
    slope_page = (slope2 * page).reshape(nrow, 1)

    pps = _pick(n_pages, (DEC_PAGES_PER_STEP, 2, 1))
    n_steps = n_pages // pps

    def page_idx(i, b, p, pt):
        return (layer, pt[b, jnp.minimum(p, n_steps - 1) * pps + i], 0, 0, 0)

    def const2(b, p, pt):
        return (0, 0)

    page_specs = [pl.BlockSpec((1, 1, page, n_heads, hw), functools.partial(page_idx, i)) for i in range(pps)]
    grid_spec = pltpu.PrefetchScalarGridSpec(
        num_scalar_prefetch=1,
        grid=(bsz, n_steps + 1),
        in_specs=[
            pl.BlockSpec((1, nrow, hw), lambda b, p, pt: (b, 0, 0)),
            *page_specs,
            *page_specs,
            pl.BlockSpec((1, page, n_heads, hw), lambda b, p, pt: (b, 0, 0, 0)),
            pl.BlockSpec((1, page, n_heads, hw), lambda b, p, pt: (b, 0, 0, 0)),
            pl.BlockSpec((nrow, ncol), const2),
            pl.BlockSpec((nrow, ncol), const2),
            pl.BlockSpec((nrow, 1), const2),
            pl.BlockSpec(lam_p.shape, const2),
            pl.BlockSpec((1, hw), const2),
        ],
        out_specs=pl.BlockSpec((1, n_tok, n_heads * hw), lambda b, p, pt: (b, 0, 0)),
        scratch_shapes=[pltpu.VMEM((nrow, 1), F32), pltpu.VMEM((nrow, 1), F32), pltpu.VMEM((nrow, hw), F32)],
    )
    out = pl.pallas_call(
        functools.partial(_attn_dec_kernel, pps=pps, n_steps=n_steps, n_tok=n_tok, n_heads=n_heads,
                          lambda_init=lambda_init),
        grid_spec=grid_spec,
        out_shape=jax.ShapeDtypeStruct((bsz, n_tok, n_heads * hw), BF16),
        compiler_params=_cparams("arbitrary", "arbitrary"),
        name="attn_sample",
    )(page_table, qs, *([cache_k] * pps), *([cache_v] * pps), kn, vn, base, base_new, slope_page,
      lam_p.astype(F32), sub_norm.reshape(1, hw).astype(F32))
    return out.reshape(bsz * n_tok, n_heads * hw)


def _ssd_params(j, dims, ssd_in, conv_w, conv_b, dt_bias, a_log, d_skip, norm_g):
    d_inner, nst, groups, heads = dims
    hpg = heads // groups
    pdim = d_inner // heads
    conv_dim = conv_w.shape[2]
    d_model = ssd_in.shape[1]

    def per_group(v):
        return jnp.pad(v.astype(F32).reshape(groups, hpg), ((0, 0), (0, LANES - hpg))).reshape(1, groups * LANES)

    w_dt = lax.slice(ssd_in, (j, 0, d_inner + conv_dim), (j + 1, d_model, ssd_in.shape[2]))
    w_dt = w_dt.reshape(d_model, groups, hpg)
    w_dt = jnp.pad(w_dt, ((0, 0), (0, 0), (0, LANES - hpg))).reshape(d_model, groups * LANES)
    return dict(
        w_dt=w_dt.astype(BF16),
        conv_w=conv_w[j].astype(F32),
        conv_b=conv_b[j].reshape(1, conv_dim).astype(F32),
        dt_bias_g=per_group(dt_bias[j]),
        a_log_g=per_group(a_log[j]),
        d_x=jnp.repeat(d_skip[j].astype(F32), pdim).reshape(1, d_inner),
        norm_g=norm_g[j].reshape(1, d_inner).astype(F32),
    )


def kernel(x_prompt, x_sample, cache_k, cache_v, state_conv, state_ssm, page_table, p_prompt, p_sample, norm_ffa, ffa_in, ffa_out, norm_mix, norm_ffb, ffb_in, ffb_out, norm_ple, ple_proj, ple_gate, ssd_in, ssd_conv_w, ssd_conv_b, ssd_dt_bias, ssd_a_log, ssd_d, ssd_norm, ssd_out, attn_qkv, attn_qk_norm, attn_lambda, attn_sub_norm, attn_out):
    bp, seq, d_model = x_prompt.shape
    bs, n_tok, _ = x_sample.shape
    depth = norm_ffa.shape[0]
    n_mixers = 2
    n_attn = depth // n_mixers
    heads = ssd_dt_bias.shape[1]
    d_inner = ssd_out.shape[1]
    nst = state_ssm.shape[-1]
    conv_dim = ssd_conv_w.shape[2]
    kw = ssd_conv_w.shape[1]
    groups = (conv_dim - d_inner) // (2 * nst)
    dims = (d_inner, nst, groups, heads)
    n_heads = cache_k.shape[3]
    hw = cache_k.shape[4]
    hd = hw // 2
    att_w = n_heads * hw
    lc = SSD_CHUNK
    assert seq % lc == 0 and n_tok <= lc and n_tok >= kw - 1 and kw - 1 <= CONV_TAIL

    n_zx = d_inner + conv_dim
    ple_proj_b, ple_gate_b = ple_proj.astype(BF16), ple_gate.astype(BF16)
    ssd_in_b, ssd_out_b = ssd_in[:, :, :n_zx].astype(BF16), ssd_out.astype(BF16)
    attn_qkv_b, attn_out_b = attn_qkv.astype(BF16), attn_out.astype(BF16)
    state_ssm = state_ssm.astype(F32)

    def ffn_both(hs, g, w_in, w_out, layer):
        h_s, wa, wb, wo = _ffn(hs[1], g, w_in, w_in, w_out, layer=layer)
        return [_ffn(hs[0], g, wa, wb, wo), h_s]

    h = [x_prompt.reshape(bp * seq, d_model), x_sample.reshape(bs * n_tok, d_model)]
    ple_in = [p_prompt.reshape(depth, bp * seq, -1), p_sample.reshape(depth, bs * n_tok, -1)]
    kp_stack = vp_stack = kv_prev = None
    ks_l, vs_l = [], []
    conv_out, ssm_out = [[], []], [[], []]

    for i in range(depth):
        j = i // n_mixers
        h = ffn_both(h, norm_ffa[i], ffa_in, ffa_out, i)
        if i % n_mixers == 0:
            prm = _ssd_params(j, dims, ssd_in, ssd_conv_w, ssd_conv_b, ssd_dt_bias, ssd_a_log, ssd_d, ssd_norm)
            zx, dtg = _ssd_inproj(h[0], norm_mix[i], ssd_in_b, j, n_zx, prm["w_dt"])
            y, h_last = _ssd_core(zx, dtg, bp, seq // lc, lc, dims, prm)
            conv_out[0].append(zx.reshape(bp, seq, -1)[:, seq - (kw - 1):, d_inner:])
            ssm_out[0].append(h_last)
            h[0] = _matmul_residual(h[0], y, ssd_out_b, j)
            zs, dts = _ssd_inproj(h[1], norm_mix[i], ssd_in_b, j, n_zx, prm["w_dt"])
            zs3 = zs.reshape(bs, n_tok, -1)
            pad_rows = ((0, 0), (0, lc - n_tok), (0, 0))
            zs_pad = jnp.pad(zs3, pad_rows).reshape(bs * lc, -1)
            dts_pad = jnp.pad(dts.reshape(bs, n_tok, -1), pad_rows).reshape(bs * lc, -1)
            tail = jnp.pad(state_conv[j].astype(F32), ((0, 0), (CONV_TAIL - (kw - 1), 0), (0, 0)))
            y, h_last = _ssd_core(zs_pad, dts_pad, bs, 1, n_tok, dims, prm, init=(tail, state_ssm, j))
            xbc = zs3[:, :, d_inner:]
            conv_out[1].append(jnp.concatenate([state_conv[j].astype(F32), xbc], axis=1)[:, n_tok:])
            ssm_out[1].append(h_last)
            y = y.reshape(bs, lc, d_inner)[:, :n_tok].reshape(bs * n_tok, d_inner)
            h[1] = _matmul_residual(h[1], y, ssd_out_b, j)
        else:
            lambda_init = 0.8 - 0.6 * math.exp(-0.3 * i)
            qk_gain = attn_qk_norm[j]
            q1, q2, kp_stack, kb, vp_stack, vt = _qkv(
                h[0], norm_mix[i], attn_qkv_b, j, att_w, qk_gain, hd, tm=_pick(seq, (512, 256, 128)),
                with_vt=True, stack=(j, n_attn, kv_prev))
            kv_prev = (kp_stack, vp_stack)
            o = _attn_prompt(q1, q2, kb, vt, bp, seq, n_heads, attn_lambda[j], attn_sub_norm[j], lambda_init)
            h[0] = _matmul_residual(h[0], o, attn_out_b, j)
            q1, q2, kf, _, vf = _qkv(h[1], norm_mix[i], attn_qkv_b, j, att_w, qk_gain, hd)
            o = _attn_sample(q1, q2, kf, vf, cache_k, cache_v, j, page_table, bs, n_tok, n_heads,
                             attn_lambda[j], attn_sub_norm[j], lambda_init)
            ks_l.append(kf.reshape(bs, n_tok, n_heads, hw))
            vs_l.append(vf.reshape(bs, n_tok, n_heads, hw))
            h[1] = _matmul_residual(h[1], o, attn_out_b, j)
        h = ffn_both(h, norm_ffb[i], ffb_in, ffb_out, i)
        h = [_ple(x, ple_in[gi][i], norm_ple[i], ple_proj_b, ple_gate_b, i) for gi, x in enumerate(h)]

    return (h[0].reshape(bp, seq, d_model), h[1].reshape(bs, n_tok, d_model),
            kp_stack.reshape(n_attn, bp, seq, n_heads, hw), vp_stack.reshape(n_attn, bp, seq, n_heads, hw),
            jnp.stack(ks_l), jnp.stack(vs_l),
            jnp.stack(conv_out[0]), jnp.stack(ssm_out[0]), jnp.stack(conv_out[1]), jnp.stack(ssm_out[1]))
```

```python
import functools
import math

import jax
import jax.numpy as jnp
from jax import lax
from jax.experimental import pallas as pl
from jax.experimental.pallas import tpu as pltpu

F32 = jnp.float32
BF16 = jnp.bfloat16
NORM_EPS = 1e-6
LOG2E = 1.4426950408889634
SSD_CHUNK = 128
LANES = 128
BF16_ROWS = 16
CONV_TAIL = 8
DEC_PAGES_PER_STEP = 4
SSD_GROUPS_PER_STEP = 2
QKV_COL_TILE = 1024
VMEM_LIMIT_BYTES = 56 * 1024 * 1024
ROW_TILES = (512, 256, 128, 64, 32, 16, 8)
BIG_ROW_TILES = (1024,) + ROW_TILES


def _cparams(*sem):
    return pltpu.CompilerParams(dimension_semantics=sem, vmem_limit_bytes=VMEM_LIMIT_BYTES)


def _pick(n, prefs):
    for t in prefs:
        if n % t == 0:
            return t
    return n


def _silu(x):
    return x * jax.nn.sigmoid(x)


def _rms(x, g):
    return x * lax.rsqrt(jnp.mean(x * x, axis=-1, keepdims=True) + NORM_EPS) * g


def _dot(a, b):
    return jnp.dot(a, b, preferred_element_type=F32)


def _dot_nt(a, b):
    return lax.dot_general(a, b, (((1,), (1,)), ((), ())), preferred_element_type=F32)


def _split3(x):
    hi = x.astype(BF16)
    r = x - hi.astype(F32)
    mid = r.astype(BF16)
    lo = (r - mid.astype(F32)).astype(BF16)
    return hi, mid, lo


def _rows_reduce(x, op):
    rows = x.shape[0]
    slabs = 8 if rows % 64 == 0 else 1
    part = op(x.reshape(slabs, rows // slabs, x.shape[1]), axis=0)
    return op(part, axis=0, keepdims=True)


def _lambda_full(lam_ref, lambda_init):
    lf = lam_ref[...]
    return (jnp.exp(jnp.sum(lf[0:1] * lf[1:2], axis=-1, keepdims=True))
            - jnp.exp(jnp.sum(lf[2:3] * lf[3:4], axis=-1, keepdims=True)) + lambda_init)


def _ffn_kernel(h_ref, g_ref, wa_ref, wb_ref, wo_ref, o_ref, *rest, emit):
    xn_ref, acc_ref = rest[-2:]
    f = pl.program_id(1)

    @pl.when(f == 0)
    def _():
        xn_ref[...] = _rms(h_ref[...], g_ref[...]).astype(BF16)
        acc_ref[...] = jnp.zeros_like(acc_ref)

    wa, wb, wo = wa_ref[...].astype(BF16), wb_ref[...].astype(BF16), wo_ref[...].astype(BF16)
    if emit:
        rest[0][...], rest[1][...], rest[2][...] = wa, wb, wo
    xn = xn_ref[...]
    a = _dot(xn, wa)
    b = _dot(xn, wb)
    acc_ref[...] += _dot((_silu(a) * b).astype(BF16), wo)

    @pl.when(f == pl.num_programs(1) - 1)
    def _():
        o_ref[...] = h_ref[...] + 0.5 * acc_ref[...]


def _ffn(h, g, wa, wb, wo, layer=None):
    m, d = h.shape
    emit = layer is not None
    dff = wo.shape[-2]
    tm = _pick(m, ROW_TILES)
    tf = _pick(dff, (512, 256, 128))
    nf = dff // tf
    if emit:
        assert m == tm
        w_specs = [pl.BlockSpec((None, d, tf), lambda i, f: (layer, 0, f)),
                   pl.BlockSpec((None, d, tf), lambda i, f: (layer, 0, f + nf)),
                   pl.BlockSpec((None, tf, d), lambda i, f: (layer, f, 0))]
    else:
        w_specs = [pl.BlockSpec((d, tf), lambda i, f: (0, f)),
                   pl.BlockSpec((d, tf), lambda i, f: (0, f)),
                   pl.BlockSpec((tf, d), lambda i, f: (f, 0))]
    out_specs = [pl.BlockSpec((tm, d), lambda i, f: (i, 0))]
    out_shape = [jax.ShapeDtypeStruct((m, d), F32)]
    if emit:
        out_specs += [pl.BlockSpec((d, tf), lambda i, f: (0, f)), pl.BlockSpec((d, tf), lambda i, f: (0, f)),
                      pl.BlockSpec((tf, d), lambda i, f: (f, 0))]
        out_shape += [jax.ShapeDtypeStruct((d, dff), BF16), jax.ShapeDtypeStruct((d, dff), BF16),
                      jax.ShapeDtypeStruct((dff, d), BF16)]
    out = pl.pallas_call(
        functools.partial(_ffn_kernel, emit=emit),
        grid=(m // tm, nf),
        in_specs=[pl.BlockSpec((tm, d), lambda i, f: (i, 0)), pl.BlockSpec((1, d), lambda i, f: (0, 0))] + w_specs,
        out_specs=out_specs,
        out_shape=out_shape,
        scratch_shapes=[pltpu.VMEM((tm, d), BF16), pltpu.VMEM((tm, d), F32)],
        compiler_params=_cparams("parallel", "arbitrary"),
        name="ffn_cast" if emit else "ffn",
    )(h, g.reshape(1, d), wa, wb, wo)
    return out if emit else out[0]


def _ple_kernel(h_ref, p_ref, g_ref, wp_ref, wg_ref, o_ref):
    h = h_ref[...]
    xn = _rms(h, g_ref[...]).astype(BF16)
    gate = jax.nn.sigmoid(_dot(xn, wg_ref[...]))
    proj = _dot(p_ref[...].astype(BF16), wp_ref[...])
    o_ref[...] = h + proj * gate


def _ple(h, p, g, w_proj, w_gate, layer):
    m, d = h.shape
    pd = p.shape[1]
    tm = _pick(m, (256, 128, 64, 32, 16, 8))
    return pl.pallas_call(
        _ple_kernel,
        grid=(m // tm,),
        in_specs=[
            pl.BlockSpec((tm, d), lambda i: (i, 0)),
            pl.BlockSpec((tm, pd), lambda i: (i, 0)),
            pl.BlockSpec((1, d), lambda i: (0, 0)),
            pl.BlockSpec((None, pd, d), lambda i: (layer, 0, 0)),
            pl.BlockSpec((None, d, d), lambda i: (layer, 0, 0)),
        ],
        out_specs=pl.BlockSpec((tm, d), lambda i: (i, 0)),
        out_shape=jax.ShapeDtypeStruct((m, d), F32),
        compiler_params=_cparams("parallel"),
        name="ple",
    )(h, p, g.reshape(1, d), w_proj, w_gate)


def _weight_spec(block, index, layer):
    if layer is None:
        return pl.BlockSpec(block, index)
    return pl.BlockSpec((None,) + block, lambda *a: (layer,) + index(*a))


def _inproj_kernel(h_ref, g_ref, w_ref, wx_ref, y_ref, yx_ref, *rest, emit):
    xn_ref = rest[-1]

    @pl.when(pl.program_id(1) == 0)
    def _():
        xn_ref[...] = _rms(h_ref[...], g_ref[...]).astype(BF16)
        yx_ref[...] = _dot(xn_ref[...], wx_ref[...])

    w = w_ref[...].astype(BF16)
    if emit:
        rest[0][...] = w
    y_ref[...] = _dot(xn_ref[...], w)


def _ssd_inproj(h, g, w, n, w_dt, layer=None):
    m, d = h.shape
    emit = layer is not None
    nx = w_dt.shape[1]
    tm = _pick(m, BIG_ROW_TILES)
    tn = _pick(n, (512, 256, 128) if tm > 512 else (1024, 512, 256, 128))
    assert not emit or m == tm
    out_specs = [pl.BlockSpec((tm, tn), lambda i, j: (i, j)), pl.BlockSpec((tm, nx), lambda i, j: (i, 0))]
    out_shape = [jax.ShapeDtypeStruct((m, n), F32), jax.ShapeDtypeStruct((m, nx), F32)]
    if emit:
        out_specs.append(pl.BlockSpec((d, tn), lambda i, j: (0, j)))
        out_shape.append(jax.ShapeDtypeStruct((d, n), BF16))
    return pl.pallas_call(
        functools.partial(_inproj_kernel, emit=emit),
        grid=(m // tm, n // tn),
        in_specs=[
            pl.BlockSpec((tm, d), lambda i, j: (i, 0)),
            pl.BlockSpec((1, d), lambda i, j: (0, 0)),
            _weight_spec((d, tn), lambda i, j: (0, j), layer),
            pl.BlockSpec((d, nx), lambda i, j: (0, 0)),
        ],
        out_specs=out_specs,
        out_shape=out_shape,
        scratch_shapes=[pltpu.VMEM((tm, d), BF16)],
        compiler_params=_cparams("parallel", "arbitrary"),
        name="ssd_inproj_cast" if emit else "ssd_inproj",
    )(h, g.reshape(1, d), w, w_dt)


def _seg_meansq(y, seg_ref, hd):
    y2 = y * y
    hi = y2.astype(BF16)
    lo = (y2 - hi.astype(F32)).astype(BF16)
    seg = seg_ref[...]
    return (_dot(hi, seg) + _dot(lo, seg)) * (1.0 / hd)


def _qkv_kernel(*refs, hd, nsec, vt_rows, n_prev, emit):
    h_ref, g_ref, w_ref, gq_ref, gk_ref, seg_ref = refs[:6]
    outs = refs[6 + n_prev:-1]
    q1_ref, q2_ref, kf_ref, kb_ref, vf_ref = outs[:5]
    xn_ref = refs[-1]
    j = pl.program_id(1)

    @pl.when(j == 0)
    def _():
        xn_ref[...] = _rms(h_ref[...], g_ref[...]).astype(BF16)

    w = w_ref[...].astype(BF16)
    if emit:
        outs[-1][...] = w
    y = _dot(xn_ref[...], w)
    cw = seg_ref.shape[0]

    def head_norm(gain_ref):
        parts = []
        for c in range(y.shape[1] // cw):
            yc = y[:, c * cw:(c + 1) * cw]
            parts.append(yc * lax.rsqrt(_seg_meansq(yc, seg_ref, hd) + NORM_EPS))
        yn = parts[0] if len(parts) == 1 else jnp.concatenate(parts, axis=1)
        return yn * gain_ref[...]

    @pl.when(j < nsec)
    def _():
        yn = head_norm(gq_ref) * (LOG2E / math.sqrt(hd))
        lane = lax.broadcasted_iota(jnp.int32, yn.shape, 1)
        first = (lane % (2 * hd)) < hd
        q1_ref[...] = jnp.where(first, yn, 0.0).astype(BF16)
        q2_ref[...] = jnp.where(first, 0.0, yn).astype(BF16)

    @pl.when((j >= nsec) & (j < 2 * nsec))
    def _():
        yn = head_norm(gk_ref)
        kf_ref[...] = yn
        kb_ref[...] = yn.astype(BF16)

    @pl.when(j >= 2 * nsec)
    def _():
        vf_ref[...] = y
        if vt_rows:
            for r in range(y.shape[0] // vt_rows):
                outs[5][r] = y[r * vt_rows:(r + 1) * vt_rows, :].T.astype(BF16)


def _qkv(h, g, w, n, qk_gain, hd, layer=None, vt_rows=None, stack=None):
    m, d = h.shape
    emit = layer is not None
    tm = _pick(m, BIG_ROW_TILES)
    tn = _pick(n, (512, 256) if tm > 512 else (QKV_COL_TILE, 512, 256))
    cw = min(tn, 256)
    nsec = n // tn
    assert (not emit or m == tm) and (not vt_rows or tm % vt_rows == 0)

    def sec(j, s):
        return jnp.clip(j - s * nsec, 0, nsec - 1)

    lane = jnp.arange(cw)
    seg = (lane[:, None] // hd == lane[None, :] // hd).astype(BF16)
    gains = [jnp.tile(qk_gain[s].reshape(1, 2 * hd).astype(F32), (1, tn // (2 * hd))) for s in range(2)]
    in_specs = [
        pl.BlockSpec((tm, d), lambda i, j: (i, 0)),
        pl.BlockSpec((1, d), lambda i, j: (0, 0)),
        _weight_spec((d, tn), lambda i, j: (0, j), layer),
        pl.BlockSpec((1, tn), lambda i, j: (0, 0)),
        pl.BlockSpec((1, tn), lambda i, j: (0, 0)),
        pl.BlockSpec((cw, cw), lambda i, j: (0, 0)),
    ]
    args = [h, g.reshape(1, d), w, gains[0], gains[1], seg]

    def tile_spec(s):
        return pl.BlockSpec((tm, tn), lambda i, j: (i, sec(j, s)))

    aliases = {}
    if stack is not None:
        slot, slots, prev = stack

        def f32_spec(s):
            return pl.BlockSpec((None, tm, tn), lambda i, j: (slot, i, sec(j, s)))

        f32_shape = jax.ShapeDtypeStruct((slots, m, n), F32)
        if prev is not None:
            aliases = {len(args): 2, len(args) + 1: 4}
            in_specs += [pl.BlockSpec(memory_space=pl.ANY)] * 2
            args += list(prev)
    else:
        f32_spec, f32_shape = tile_spec, jax.ShapeDtypeStruct((m, n), F32)
    out_specs = [tile_spec(0), tile_spec(0), f32_spec(1), tile_spec(1), f32_spec(2)]
    out_shape = [jax.ShapeDtypeStruct((m, n), BF16)] * 2 + [f32_shape, jax.ShapeDtypeStruct((m, n), BF16),
                                                            f32_shape]
    if vt_rows:
        out_specs.append(pl.BlockSpec((tm // vt_rows, tn, vt_rows), lambda i, j: (i, sec(j, 2), 0)))
        out_shape.append(jax.ShapeDtypeStruct((m // vt_rows, n, vt_rows), BF16))
    if emit:
        out_specs.append(pl.BlockSpec((d, tn), lambda i, j: (0, j)))
        out_shape.append(jax.ShapeDtypeStruct((d, 3 * n), BF16))
    return pl.pallas_call(
        functools.partial(_qkv_kernel, hd=hd, nsec=nsec, vt_rows=vt_rows, n_prev=len(aliases), emit=emit),
        grid=(m // tm, 3 * nsec),
        in_specs=in_specs,
        out_specs=out_specs,
        out_shape=out_shape,
        scratch_shapes=[pltpu.VMEM((tm, d), BF16)],
        input_output_aliases=aliases,
        compiler_params=_cparams("parallel", "arbitrary"),
        name="qkv_proj_cast" if emit else "qkv_proj",
    )(*args)


def _mm_res_kernel(h_ref, x_ref, w_ref, o_ref, *rest, emit):
    w = w_ref[...].astype(BF16)
    if emit:
        rest[0][...] = w
    o_ref[...] = h_ref[...] + _dot(x_ref[...].astype(BF16), w)


def _matmul_residual(h, x, w, layer=None):
    m, n = h.shape
    k = x.shape[1]
    emit = layer is not None
    tm = _pick(m, BIG_ROW_TILES)
    tn = _pick(n, (512, 256, 128))
    assert not emit or m == tm
    out_specs = [pl.BlockSpec((tm, tn), lambda i, j: (i, j))]
    out_shape = [jax.ShapeDtypeStruct((m, n), F32)]
    if emit:
        out_specs.append(pl.BlockSpec((k, tn), lambda i, j: (0, j)))
        out_shape.append(jax.ShapeDtypeStruct((k, n), BF16))
    out = pl.pallas_call(
        functools.partial(_mm_res_kernel, emit=emit),
        grid=(m // tm, n // tn),
        in_specs=[
            pl.BlockSpec((tm, tn), lambda i, j: (i, j)),
            pl.BlockSpec((tm, k), lambda i, j: (i, 0)),
            _weight_spec((k, tn), lambda i, j: (0, j), layer),
        ],
        out_specs=out_specs,
        out_shape=out_shape,
        compiler_params=_cparams("parallel", "arbitrary"),
        name="matmul_residual_cast" if emit else "matmul_residual",
    )(h, x, w)
    return out if emit else out[0]


def _ssd_kernel(*refs, lc, valid, has_init, hpg, pdim, gps):
    (z_ref, x_ref, b_ref, c_ref, dt_ref) = refs[:5]
    pos = 5
    if has_init:
        tx_ref, tb_ref, tc_ref, h0_ref = refs[pos:pos + 4]
        pos += 4
    (cwx_ref, cwb_ref, cwc_ref, cbx_ref, cbb_ref, cbc_ref,
     dtb_ref, alog_ref, dsk_ref, ng_ref, hx_ref) = refs[pos:pos + 11]
    pos += 11
    y_ref, hout_ref = refs[pos:pos + 2]
    ht_ref, ex_ref, eb_ref, ec_ref = refs[pos + 2:]

    c = pl.program_id(2)
    w = hpg * pdim
    nst = ht_ref.shape[0]
    t0 = CONV_TAIL

    @pl.when(c == 0)
    def _():
        if has_init:
            ex_ref[0:t0, :] = tx_ref[0]
            eb_ref[0:t0, :] = tb_ref[0]
            ec_ref[0:t0, :] = tc_ref[0]
            ht_ref[...] = h0_ref[0].reshape(gps * w, nst).T
        else:
            ex_ref[0:t0, :] = jnp.zeros((t0, gps * w), F32)
            eb_ref[0:t0, :] = jnp.zeros((t0, gps * nst), F32)
            ec_ref[0:t0, :] = jnp.zeros((t0, gps * nst), F32)
            ht_ref[...] = jnp.zeros_like(ht_ref)

    def conv_act(e_ref, raw_ref, cw_ref, cb_ref, lanes):
        e_ref[t0:t0 + lc, lanes] = raw_ref[:, lanes]
        kw = cw_ref.shape[0]
        acc = cb_ref[:, lanes] + cw_ref[kw - 1:kw, lanes] * e_ref[t0:t0 + lc, lanes]
        for k in range(kw - 1):
            off = t0 - (kw - 1) + k
            acc = acc + cw_ref[k:k + 1, lanes] * e_ref[off:off + lc, lanes]
        e_ref[0:t0, lanes] = e_ref[lc:lc + t0, lanes]
        return _silu(acc)

    row = lax.broadcasted_iota(jnp.int32, (lc, lc), 0)
    col = lax.broadcasted_iota(jnp.int32, (lc, lc), 1)
    causal = row >= col
    tri = causal.astype(BF16)
    lane = lax.broadcasted_iota(jnp.int32, (lc, LANES), 1)
    first_half = lane < pdim

    def one_group(gi):
        wl = slice(gi * w, (gi + 1) * w)
        nl = slice(gi * nst, (gi + 1) * nst)
        hl = slice(gi * LANES, (gi + 1) * LANES)
        xc = conv_act(ex_ref, x_ref, cwx_ref, cbx_ref, wl)
        bc = conv_act(eb_ref, b_ref, cwb_ref, cbb_ref, nl)
        cc = conv_act(ec_ref, c_ref, cwc_ref, cbc_ref, nl)

        dtr = dt_ref[:, hl] + dtb_ref[:, hl]
        dt = jnp.maximum(dtr, 0.0) + jnp.log1p(jnp.exp(-jnp.abs(dtr)))
        if valid < lc:
            dt = jnp.where(lax.broadcasted_iota(jnp.int32, (lc, LANES), 0) < valid, dt, 0.0)
        da = dt * (-jnp.exp(alog_ref[:, hl]))
        d_hi, d_mid, d_lo = _split3(da)
        cs = _dot(tri, d_hi) + _dot(tri, d_mid) + _dot(tri, d_lo)
        cs_t = cs.T
        ecs = jnp.exp(cs)
        coef_s = dt * jnp.exp(cs[lc - 1:lc, :] - cs)

        e_hi, e_mid, e_lo = _split3(ecs)
        ex = _dot(jnp.concatenate([dt.astype(BF16), coef_s.astype(BF16), e_hi, e_mid, e_lo], axis=0),
                  hx_ref[...])
        dt_x, coef_x = ex[0:lc], ex[lc:2 * lc]
        ecs_x = ex[2 * lc:3 * lc] + ex[3 * lc:4 * lc] + ex[4 * lc:5 * lc]

        bcb = bc.astype(BF16)
        ccb = cc.astype(BF16)
        cb = _dot_nt(ccb, bcb)
        xd = xc * dt_x

        def decay_mat(j):
            col_j = jnp.sum(jnp.where(lane == j, cs, 0.0), axis=-1, keepdims=True)
            return (cb * jnp.exp(jnp.where(causal, col_j - cs_t[j:j + 1, :], -jnp.inf))).astype(BF16)

        y_blks = []
        for q in range(hpg // 2):
            mcat = jnp.concatenate([decay_mat(2 * q), decay_mat(2 * q + 1)], axis=1)
            xp = xd[:, q * LANES:(q + 1) * LANES]
            rhs = jnp.concatenate([jnp.where(first_half, xp, 0.0), jnp.where(first_half, 0.0, xp)],
                                  axis=0).astype(BF16)
            y_blks.append(_dot(mcat, rhs))
        y = y_blks[0] if len(y_blks) == 1 else jnp.concatenate(y_blks, axis=1)

        ht_in = ht_ref[:, wl]
        y = y + _dot(ccb, ht_in.astype(BF16)) * ecs_x
        upd = _dot(bc.T.astype(BF16), (xc * coef_x).astype(BF16))
        ht_ref[:, wl] = ht_in * ecs_x[lc - 1:lc, :] + upd

        y = y + xc * dsk_ref[:, wl]
        y = y * _silu(z_ref[:, wl])
        y = y * lax.rsqrt(jnp.mean(y * y, axis=-1, keepdims=True) + NORM_EPS) * ng_ref[:, wl]
        y_ref[:, wl] = y.astype(BF16)

    for gi in range(gps):
        one_group(gi)

    @pl.when(c == pl.num_programs(2) - 1)
    def _():
        hout_ref[0] = ht_ref[...].T.reshape(gps * hpg, pdim, nst)


def _ssd_core(zx, dtg, bsz, n_chunks, valid, dims, prm, init=None):
    d_inner, nst, groups, heads = dims
    hpg = heads // groups
    pdim = d_inner // heads
    w = hpg * pdim
    lc = SSD_CHUNK
    assert pdim * 2 == LANES and hpg % 2 == 0 and nst == LANES
    gps = _pick(groups, (SSD_GROUPS_PER_STEP, 1))
    gw, gn, gl = gps * w, gps * nst, gps * LANES
    rows = bsz * n_chunks * lc
    xo, bo, co = d_inner // gw, 2 * d_inner // gn, (2 * d_inner + groups * nst) // gn
    cbo, cco = d_inner // gn, (d_inner + groups * nst) // gn
    head_expand = (jnp.arange(LANES)[:, None] == jnp.arange(w)[None, :] // pdim).astype(BF16)

    def rowblk(b, g, c):
        return b * n_chunks + c

    in_specs = [
        pl.BlockSpec((lc, gw), lambda b, g, c: (rowblk(b, g, c), g)),
        pl.BlockSpec((lc, gw), lambda b, g, c: (rowblk(b, g, c), xo + g)),
        pl.BlockSpec((lc, gn), lambda b, g, c: (rowblk(b, g, c), bo + g)),
        pl.BlockSpec((lc, gn), lambda b, g, c: (rowblk(b, g, c), co + g)),
        pl.BlockSpec((lc, gl), lambda b, g, c: (rowblk(b, g, c), g)),
    ]
    args = [zx, zx, zx, zx, dtg]
    if init is not None:
        tail, h0, h0_layer = init
        in_specs += [
            pl.BlockSpec((1, CONV_TAIL, gw), lambda b, g, c: (b, 0, g)),
            pl.BlockSpec((1, CONV_TAIL, gn), lambda b, g, c: (b, 0, cbo + g)),
            pl.BlockSpec((1, CONV_TAIL, gn), lambda b, g, c: (b, 0, cco + g)),
            pl.BlockSpec((None, 1, gps * hpg, pdim, nst), lambda b, g, c: (h0_layer, b, g, 0, 0)),
        ]
        args += [tail, tail, tail, h0]
    kw = prm["conv_w"].shape[0]
    in_specs += [
        pl.BlockSpec((kw, gw), lambda b, g, c: (0, g)),
        pl.BlockSpec((kw, gn), lambda b, g, c: (0, cbo + g)),
        pl.BlockSpec((kw, gn), lambda b, g, c: (0, cco + g)),
        pl.BlockSpec((1, gw), lambda b, g, c: (0, g)),
        pl.BlockSpec((1, gn), lambda b, g, c: (0, cbo + g)),
        pl.BlockSpec((1, gn), lambda b, g, c: (0, cco + g)),
        pl.BlockSpec((1, gl), lambda b, g, c: (0, g)),
        pl.BlockSpec((1, gl), lambda b, g, c: (0, g)),
        pl.BlockSpec((1, gw), lambda b, g, c: (0, g)),
        pl.BlockSpec((1, gw), lambda b, g, c: (0, g)),
        pl.BlockSpec((LANES, w), lambda b, g, c: (0, 0)),
    ]
    args += [prm["conv_w"], prm["conv_w"], prm["conv_w"], prm["conv_b"], prm["conv_b"], prm["conv_b"],
             prm["dt_bias_g"], prm["a_log_g"], prm["d_x"], prm["norm_g"], head_expand]
    y, h_last = pl.pallas_call(
        functools.partial(_ssd_kernel, lc=lc, valid=valid, has_init=init is not None, hpg=hpg, pdim=pdim,
                          gps=gps),
        grid=(bsz, groups // gps, n_chunks),
        in_specs=in_specs,
        out_specs=[
            pl.BlockSpec((lc, gw), lambda b, g, c: (rowblk(b, g, c), g)),
            pl.BlockSpec((1, gps * hpg, pdim, nst), lambda b, g, c: (b, g, 0, 0)),
        ],
        out_shape=[jax.ShapeDtypeStruct((rows, d_inner), BF16),
                   jax.ShapeDtypeStruct((bsz, heads, pdim, nst), F32)],
        scratch_shapes=[pltpu.VMEM((nst, gw), F32),
                        pltpu.VMEM((lc + CONV_TAIL, gw), F32),
                        pltpu.VMEM((lc + CONV_TAIL, gn), F32),
                        pltpu.VMEM((lc + CONV_TAIL, gn), F32)],
        compiler_params=_cparams("arbitrary", "arbitrary", "arbitrary"),
        name="ssd_scan",
    )(*args)
    return y, h_last


def _attn_kernel(q1_ref, q2_ref, k_ref, vt_ref, slope_ref, lam_ref, sn_ref, o_ref,
                 m1_ref, a1_ref, m2_ref, a2_ref, *, tq, n_q, lambda_init):
    qi = pl.program_id(2)
    hw = k_ref.shape[1]
    slope2 = slope_ref[0][:, 0:1] * LOG2E
    key_row = lax.broadcasted_iota(jnp.int32, (tq, LANES), 0).astype(F32)
    streams = ((q1_ref, m1_ref, a1_ref), (q2_ref, m2_ref, a2_ref))
    for _, m_ref, a_ref in streams:
        m_ref[...] = jnp.full_like(m_ref, -jnp.inf)
        a_ref[...] = jnp.zeros_like(a_ref)
    ones_rows = jnp.ones((a1_ref.shape[0] - hw, tq), BF16)

    def tile(ki, masked):
        k = k_ref[ki * tq:(ki + 1) * tq, :]
        vt = jnp.concatenate([vt_ref[ki], ones_rows], axis=0)
        kb = slope2 * (key_row + float(ki * tq))
        kb = jnp.concatenate([kb] * (tq // LANES), axis=1)
        scores = [_dot_nt(k, q_ref[...]) + kb for q_ref, _, _ in streams]
        if masked:
            keep = (lax.broadcasted_iota(jnp.int32, (tq, tq), 0)
                    <= lax.broadcasted_iota(jnp.int32, (tq, tq), 1))
            scores = [jnp.where(keep, s, -jnp.inf) for s in scores]
        m_olds = [m_ref[...] for _, m_ref, _ in streams]
        m_news = [jnp.maximum(m_old, _rows_reduce(s, jnp.max)) for m_old, s in zip(m_olds, scores)]
        probs = [jnp.exp2((s - m_new).astype(BF16)) for s, m_new in zip(scores, m_news)]
        pvs = [_dot(vt, p) for p in probs]
        for (_, m_ref, a_ref), m_old, m_new, pv in zip(streams, m_olds, m_news, pvs):
            a_ref[...] = jnp.exp2(m_old - m_new) * a_ref[...] + pv
            m_ref[...] = m_new

    for v in range(n_q):
        @pl.when(qi == v)
        def _(v=v):
            for ki in range(v):
                tile(ki, False)
            tile(v, True)

    lam = _lambda_full(lam_ref, lambda_init)
    ot = (a1_ref[0:hw, :] / a1_ref[hw:hw + 1, :]
          - lam * (a2_ref[0:hw, :] / a2_ref[hw:hw + 1, :]))
    ot = ot * lax.rsqrt(jnp.mean(ot * ot, axis=0, keepdims=True) + NORM_EPS)
    o_ref[...] = (ot.T * (sn_ref[...] * (1.0 - lambda_init))).astype(BF16)


def _alibi_slopes(n_heads):
    return jnp.exp2(-8.0 * jnp.arange(1, n_heads + 1, dtype=F32) / n_heads)


def _attn_prompt(q1, q2, kb, vt, bsz, seq, n_heads, lam_p, sub_norm, lambda_init):
    m, width = q1.shape
    hw = width // n_heads
    tq = vt.shape[2]
    nq = seq // tq
    assert seq % tq == 0 and tq % LANES == 0
    slopes = jnp.broadcast_to(_alibi_slopes(n_heads)[:, None, None], (n_heads, 1, LANES))
    return pl.pallas_call(
        functools.partial(_attn_kernel, tq=tq, n_q=nq, lambda_init=lambda_init),
        grid=(bsz, n_heads, nq),
        in_specs=[
            pl.BlockSpec((tq, hw), lambda b, h, i: (b * nq + i, h)),
            pl.BlockSpec((tq, hw), lambda b, h, i: (b * nq + i, h)),
            pl.BlockSpec((seq, hw), lambda b, h, i: (b, h)),
            pl.BlockSpec((nq, hw, tq), lambda b, h, i: (b, h, 0)),
            pl.BlockSpec((1, 1, LANES), lambda b, h, i: (h, 0, 0)),
            pl.BlockSpec(lam_p.shape, lambda b, h, i: (0, 0)),
            pl.BlockSpec((1, hw), lambda b, h, i: (0, 0)),
        ],
        out_specs=pl.BlockSpec((tq, hw), lambda b, h, i: (b * nq + i, h)),
        out_shape=jax.ShapeDtypeStruct((m, width), BF16),
        scratch_shapes=[pltpu.VMEM((1, tq), F32), pltpu.VMEM((hw + BF16_ROWS, tq), F32),
                        pltpu.VMEM((1, tq), F32), pltpu.VMEM((hw + BF16_ROWS, tq), F32)],
        compiler_params=_cparams("parallel", "parallel", "arbitrary"),
        name="attn_prompt",
    )(q1, q2, kb, vt, slopes, lam_p.astype(F32), sub_norm.reshape(1, hw).astype(F32))


def _attn_dec_kernel(*refs, pps, n_steps, n_tok, n_heads, lambda_init):
    q_ref = refs[1]
    kp_refs, vp_refs = refs[2:2 + pps], refs[2 + pps:2 + 2 * pps]
    (kn_ref, vn_ref, base_ref, basen_ref, slc_ref, lam_ref, sn_ref,
     o_ref, m_ref, l_ref, acc_ref) = refs[2 + 2 * pps:]
    st = pl.program_id(1)
    rph = 2 * n_tok

    @pl.when(st == 0)
    def _():
        m_ref[...] = jnp.full_like(m_ref, -jnp.inf)
        l_ref[...] = jnp.zeros_like(l_ref)
        acc_ref[...] = jnp.zeros_like(acc_ref)

    q = q_ref[0]

    def flat(x3):
        return x3.reshape(x3.shape[0] * x3.shape[1], x3.shape[2]).astype(BF16)

    def update(pages, bias_ref, shifts):
        m_old = m_ref[...]
        m_new = m_old
        scores = []
        for (k_ref3, _), shift in zip(pages, shifts):
            s = _dot_nt(q, flat(k_ref3())) + bias_ref[...]
            scores.append(s)
            m_new = jnp.maximum(m_new, jnp.max(s, axis=-1, keepdims=True) + shift)
        alpha = jnp.exp2(m_old - m_new)
        l_new = alpha * l_ref[...]
        acc = alpha * acc_ref[...]
        for s, (_, v_ref3), shift in zip(scores, pages, shifts):
            p = jnp.exp2(s - (m_new - shift))
            l_new = l_new + jnp.sum(p, axis=-1, keepdims=True)
            acc = acc + _dot(p.astype(BF16), flat(v_ref3()))
        l_ref[...] = l_new
        acc_ref[...] = acc
        m_ref[...] = m_new

    @pl.when(st < n_steps)
    def _():
        first = (st * pps).astype(F32)
        update([(lambda r=kp_refs[i]: r[0, 0], lambda r=vp_refs[i]: r[0, 0]) for i in range(pps)],
               base_ref, [slc_ref[...] * (first + float(i)) for i in range(pps)])

    @pl.when(st == n_steps)
    def _():
        update([(lambda: kn_ref[0], lambda: vn_ref[0])], basen_ref, [0.0])
        lam = _lambda_full(lam_ref, lambda_init)
        on = acc_ref[...] / l_ref[...]
        hw = on.shape[1]
        for h in range(n_heads):
            o = on[h * rph:h * rph + n_tok, :] - lam * on[h * rph + n_tok:(h + 1) * rph, :]
            o = o * lax.rsqrt(jnp.mean(o * o, axis=-1, keepdims=True) + NORM_EPS)
            o_ref[0, :, h * hw:(h + 1) * hw] = (o * (sn_ref[...] * (1.0 - lambda_init))).astype(o_ref.dtype)


def _attn_sample(q1, q2, k_new, v_new, cache_k, cache_v, layer, page_table, bsz, n_tok, n_heads,
                 lam_p, sub_norm, lambda_init):
    page, hw = cache_k.shape[2], cache_k.shape[4]
    n_pages = page_table.shape[1]
    past = n_pages * page
    rph = 2 * n_tok
    nrow = rph * n_heads
    ncol = page * n_heads
    assert n_tok <= page and rph % 8 == 0 and n_heads % 8 == 0
    q12 = jnp.stack([q1, q2]).reshape(2, bsz, n_tok, n_heads, hw)
    qs = jnp.transpose(q12, (1, 3, 0, 2, 4)).reshape(bsz, nrow, hw)
    pad = ((0, 0), (0, page - n_tok), (0, 0), (0, 0))
    kn = jnp.pad(k_new.reshape(bsz, n_tok, n_heads, hw), pad)
    vn = jnp.pad(v_new.reshape(bsz, n_tok, n_heads, hw), pad)
    r = jnp.arange(nrow)
    c = jnp.arange(ncol)
    row_head, row_tok = r // rph, (r % n_tok).astype(F32)
    col_key, col_head = (c // n_heads).astype(F32), c % n_heads
    slope2 = (_alibi_slopes(n_heads) * LOG2E)[row_head]
    same_head = row_head[:, None] == col_head[None, :]
    dist_page = past + row_tok[:, None] - col_key[None, :]
    base = jnp.where(same_head, -slope2[:, None] * dist_page, -jnp.inf)
    dist_new = row_tok[:, None] - col_key[None, :]
    ok_new = same_head & (col_key[None, :] < n_tok) & (dist_new >= 0)
    base_new = jnp.where(ok_new, -slope2[:, None] * dist_new, -jnp.inf)
    slope_page = (slope2 * page).reshape(nrow, 1)

    pps = _pick(n_pages, (DEC_PAGES_PER_STEP, 2, 1))
    n_steps = n_pages // pps

    def page_idx(i, b, p, pt):
        return (layer, pt[b, jnp.minimum(p, n_steps - 1) * pps + i], 0, 0, 0)

    def const2(b, p, pt):
        return (0, 0)

    page_specs = [pl.BlockSpec((1, 1, page, n_heads, hw), functools.partial(page_idx, i)) for i in range(pps)]
    grid_spec = pltpu.PrefetchScalarGridSpec(
        num_scalar_prefetch=1,
        grid=(bsz, n_steps + 1),
        in_specs=[
            pl.BlockSpec((1, nrow, hw), lambda b, p, pt: (b, 0, 0)),
            *page_specs,
            *page_specs,
            pl.BlockSpec((1, page, n_heads, hw), lambda b, p, pt: (b, 0, 0, 0)),
            pl.BlockSpec((1, page, n_heads, hw), lambda b, p, pt: (b, 0, 0, 0)),
            pl.BlockSpec((nrow, ncol), const2),
            pl.BlockSpec((nrow, ncol), const2),
            pl.BlockSpec((nrow, 1), const2),
            pl.BlockSpec(lam_p.shape, const2),
            pl.BlockSpec((1, hw), const2),
        ],
        out_specs=pl.BlockSpec((1, n_tok, n_heads * hw), lambda b, p, pt: (b, 0, 0)),
        scratch_shapes=[pltpu.VMEM((nrow, 1), F32), pltpu.VMEM((nrow, 1), F32), pltpu.VMEM((nrow, hw), F32)],
    )
    out = pl.pallas_call(
        functools.partial(_attn_dec_kernel, pps=pps, n_steps=n_steps, n_tok=n_tok, n_heads=n_heads,
                          lambda_init=lambda_init),
        grid_spec=grid_spec,
        out_shape=jax.ShapeDtypeStruct((bsz, n_tok, n_heads * hw), BF16),
        compiler_params=_cparams("arbitrary", "arbitrary"),
        name="attn_sample",
    )(page_table, qs, *([cache_k] * pps), *([cache_v] * pps), kn, vn, base, base_new, slope_page,
      lam_p.astype(F32), sub_norm.reshape(1, hw).astype(F32))
    return out.reshape(bsz * n_tok, n_heads * hw)


def _ssd_params(j, dims, ssd_in, conv_w, conv_b, dt_bias, a_log, d_skip, norm_g):
    d_inner, nst, groups, heads = dims
    hpg = heads // groups
    pdim = d_inner // heads
    conv_dim = conv_w.shape[2]
    d_model = ssd_in.shape[1]

    def per_group(v):
        return jnp.pad(v.astype(F32).reshape(groups, hpg), ((0, 0), (0, LANES - hpg))).reshape(1, groups * LANES)

    w_dt = lax.slice(ssd_in, (j, 0, d_inner + conv_dim), (j + 1, d_model, ssd_in.shape[2]))
    w_dt = w_dt.reshape(d_model, groups, hpg)
    w_dt = jnp.pad(w_dt, ((0, 0), (0, 0), (0, LANES - hpg))).reshape(d_model, groups * LANES)
    return dict(
        w_dt=w_dt.astype(BF16),
        conv_w=conv_w[j].astype(F32),
        conv_b=conv_b[j].reshape(1, conv_dim).astype(F32),
        dt_bias_g=per_group(dt_bias[j]),
        a_log_g=per_group(a_log[j]),
        d_x=jnp.repeat(d_skip[j].astype(F32), pdim).reshape(1, d_inner),
        norm_g=norm_g[j].reshape(1, d_inner).astype(F32),
    )


def kernel(x_prompt, x_sample, cache_k, cache_v, state_conv, state_ssm, page_table, p_prompt, p_sample, norm_ffa, ffa_in, ffa_out, norm_mix, norm_ffb, ffb_in, ffb_out, norm_ple, ple_proj, ple_gate, ssd_in, ssd_conv_w, ssd_conv_b, ssd_dt_bias, ssd_a_log, ssd_d, ssd_norm, ssd_out, attn_qkv, attn_qk_norm, attn_lambda, attn_sub_norm, attn_out):
    bp, seq, d_model = x_prompt.shape
    bs, n_tok, _ = x_sample.shape
    depth = norm_ffa.shape[0]
    n_mixers = 2
    n_attn = depth // n_mixers
    heads = ssd_dt_bias.shape[1]
    d_inner = ssd_out.shape[1]
    nst = state_ssm.shape[-1]
    conv_dim = ssd_conv_w.shape[2]
    kw = ssd_conv_w.shape[1]
    groups = (conv_dim - d_inner) // (2 * nst)
    dims = (d_inner, nst, groups, heads)
    n_heads = cache_k.shape[3]
    hw = cache_k.shape[4]
    hd = hw // 2
    att_w = n_heads * hw
    lc = SSD_CHUNK
    assert seq % lc == 0 and n_tok <= lc and n_tok >= kw - 1 and kw - 1 <= CONV_TAIL

    n_zx = d_inner + conv_dim
    ple_proj_b, ple_gate_b = ple_proj.astype(BF16), ple_gate.astype(BF16)
    state_ssm = state_ssm.astype(F32)

    def ffn_both(hs, g, w_in, w_out, layer):
        h_s, wa, wb, wo = _ffn(hs[1], g, w_in, w_in, w_out, layer=layer)
        return [_ffn(hs[0], g, wa, wb, wo), h_s]

    h = [x_prompt.reshape(bp * seq, d_model), x_sample.reshape(bs * n_tok, d_model)]
    ple_in = [p_prompt.reshape(depth, bp * seq, -1), p_sample.reshape(depth, bs * n_tok, -1)]
    kp_stack = vp_stack = kv_prev = None
    ks_l, vs_l = [], []
    conv_out, ssm_out = [[], []], [[], []]

    for i in range(depth):
        j = i // n_mixers
        h = ffn_both(h, norm_ffa[i], ffa_in, ffa_out, i)
        if i % n_mixers == 0:
            prm = _ssd_params(j, dims, ssd_in, ssd_conv_w, ssd_conv_b, ssd_dt_bias, ssd_a_log, ssd_d, ssd_norm)
            zs, dts, w_in_b = _ssd_inproj(h[1], norm_mix[i], ssd_in, n_zx, prm["w_dt"], layer=j)
            zs3 = zs.reshape(bs, n_tok, -1)
            pad_rows = ((0, 0), (0, lc - n_tok), (0, 0))
            zs_pad = jnp.pad(zs3, pad_rows).reshape(bs * lc, -1)
            dts_pad = jnp.pad(dts.reshape(bs, n_tok, -1), pad_rows).reshape(bs * lc, -1)
            tail = jnp.pad(state_conv[j].astype(F32), ((0, 0), (CONV_TAIL - (kw - 1), 0), (0, 0)))
            y, h_last = _ssd_core(zs_pad, dts_pad, bs, 1, n_tok, dims, prm, init=(tail, state_ssm, j))
            xbc = zs3[:, :, d_inner:]
            conv_out[1].append(jnp.concatenate([state_conv[j].astype(F32), xbc], axis=1)[:, n_tok:])
            ssm_out[1].append(h_last)
            y = y.reshape(bs, lc, d_inner)[:, :n_tok].reshape(bs * n_tok, d_inner)
            h[1], w_out_b = _matmul_residual(h[1], y, ssd_out, layer=j)
            zx, dtg = _ssd_inproj(h[0], norm_mix[i], w_in_b, n_zx, prm["w_dt"])
            y, h_last = _ssd_core(zx, dtg, bp, seq // lc, lc, dims, prm)
            conv_out[0].append(zx.reshape(bp, seq, -1)[:, seq - (kw - 1):, d_inner:])
            ssm_out[0].append(h_last)
            h[0] = _matmul_residual(h[0], y, w_out_b)
        else:
            lambda_init = 0.8 - 0.6 * math.exp(-0.3 * i)
            qk_gain = attn_qk_norm[j]
            q1, q2, kf, _, vf, w_qkv_b = _qkv(h[1], norm_mix[i], attn_qkv, att_w, qk_gain, hd, layer=j)
            o = _attn_sample(q1, q2, kf, vf, cache_k, cache_v, j, page_table, bs, n_tok, n_heads,
                             attn_lambda[j], attn_sub_norm[j], lambda_init)
            ks_l.append(kf.reshape(bs, n_tok, n_heads, hw))
            vs_l.append(vf.reshape(bs, n_tok, n_heads, hw))
            h[1], w_out_b = _matmul_residual(h[1], o, attn_out, layer=j)
            q1, q2, kp_stack, kb, vp_stack, vt = _qkv(
                h[0], norm_mix[i], w_qkv_b, att_w, qk_gain, hd, vt_rows=_pick(seq, (512, 256, 128)),
                stack=(j, n_attn, kv_prev))
            kv_prev = (kp_stack, vp_stack)
            o = _attn_prompt(q1, q2, kb, vt, bp, seq, n_heads, attn_lambda[j], attn_sub_norm[j], lambda_init)
            h[0] = _matmul_residual(h[0], o, w_out_b)
        h = ffn_both(h, norm_ffb[i], ffb_in, ffb_out, i)
        h = [_ple(x, ple_in[gi][i], norm_ple[i], ple_proj_b, ple_gate_b, i) for gi, x in enumerate(h)]

    return (h[0].reshape(bp, seq, d_model), h[1].reshape(bs, n_tok, d_model),
            kp_stack.reshape(n_attn, bp, seq, n_heads, hw), vp_stack.reshape(n_attn, bp, seq, n_heads, hw),
            jnp.stack(ks_l), jnp.stack(vs_l),
            jnp.stack(conv_out[0]), jnp.stack(ssm_out[0]), jnp.stack(conv_out[1]), jnp.stack(ssm_out[1]))
```

```python
import functools
import math

import jax
import jax.numpy as jnp
from jax import lax
from jax.experimental import pallas as pl
from jax.experimental.pallas import tpu as pltpu

F32 = jnp.float32
BF16 = jnp.bfloat16
NORM_EPS = 1e-6
LOG2E = 1.4426950408889634
SSD_CHUNK = 128
LANES = 128
BF16_ROWS = 16
CONV_TAIL = 8
DEC_PAGES_PER_STEP = 4
SSD_GROUPS_PER_STEP = 4
QKV_COL_TILE = 1024
VMEM_LIMIT_BYTES = 56 * 1024 * 1024
ROW_TILES = (512, 256, 128, 64, 32, 16, 8)
BIG_ROW_TILES = (1024,) + ROW_TILES


def _cparams(*sem):
    return pltpu.CompilerParams(dimension_semantics=sem, vmem_limit_bytes=VMEM_LIMIT_BYTES)


def _pick(n, prefs):
    for t in prefs:
        if n % t == 0:
            return t
    return n


def _silu(x):
    return x * jax.nn.sigmoid(x)


def _rms(x, g):
    return x * lax.rsqrt(jnp.mean(x * x, axis=-1, keepdims=True) + NORM_EPS) * g


def _dot(a, b):
    return jnp.dot(a, b, preferred_element_type=F32)


def _dot_nt(a, b):
    return lax.dot_general(a, b, (((1,), (1,)), ((), ())), preferred_element_type=F32)


def _split3(x):
    hi = x.astype(BF16)
    r = x - hi.astype(F32)
    mid = r.astype(BF16)
    lo = (r - mid.astype(F32)).astype(BF16)
    return hi, mid, lo


def _rows_reduce(x, op):
    rows = x.shape[0]
    slabs = 8 if rows % 64 == 0 else 1
    part = op(x.reshape(slabs, rows // slabs, x.shape[1]), axis=0)
    return op(part, axis=0, keepdims=True)


def _lambda_full(lam_ref, lambda_init):
    lf = lam_ref[...]
    return (jnp.exp(jnp.sum(lf[0:1] * lf[1:2], axis=-1, keepdims=True))
            - jnp.exp(jnp.sum(lf[2:3] * lf[3:4], axis=-1, keepdims=True)) + lambda_init)


def _ffn_kernel(h_ref, g_ref, wa_ref, wb_ref, wo_ref, o_ref, *rest, emit):
    xn_ref, acc_ref = rest[-2:]
    f = pl.program_id(1)

    @pl.when(f == 0)
    def _():
        xn_ref[...] = _rms(h_ref[...], g_ref[...]).astype(BF16)
        acc_ref[...] = jnp.zeros_like(acc_ref)

    wa, wb, wo = wa_ref[...].astype(BF16), wb_ref[...].astype(BF16), wo_ref[...].astype(BF16)
    if emit:
        rest[0][...], rest[1][...], rest[2][...] = wa, wb, wo
    xn = xn_ref[...]
    a = _dot(xn, wa)
    b = _dot(xn, wb)
    acc_ref[...] += _dot((_silu(a) * b).astype(BF16), wo)

    @pl.when(f == pl.num_programs(1) - 1)
    def _():
        o_ref[...] = h_ref[...] + 0.5 * acc_ref[...]


def _ffn(h, g, wa, wb, wo, layer=None):
    m, d = h.shape
    emit = layer is not None
    dff = wo.shape[-2]
    tm = _pick(m, ROW_TILES)
    tf = _pick(dff, (512, 256, 128))
    nf = dff // tf
    if emit:
        assert m == tm
        w_specs = [pl.BlockSpec((None, d, tf), lambda i, f: (layer, 0, f)),
                   pl.BlockSpec((None, d, tf), lambda i, f: (layer, 0, f + nf)),
                   pl.BlockSpec((None, tf, d), lambda i, f: (layer, f, 0))]
    else:
        w_specs = [pl.BlockSpec((d, tf), lambda i, f: (0, f)),
                   pl.BlockSpec((d, tf), lambda i, f: (0, f)),
                   pl.BlockSpec((tf, d), lambda i, f: (f, 0))]
    out_specs = [pl.BlockSpec((tm, d), lambda i, f: (i, 0))]
    out_shape = [jax.ShapeDtypeStruct((m, d), F32)]
    if emit:
        out_specs += [pl.BlockSpec((d, tf), lambda i, f: (0, f)), pl.BlockSpec((d, tf), lambda i, f: (0, f)),
                      pl.BlockSpec((tf, d), lambda i, f: (f, 0))]
        out_shape += [jax.ShapeDtypeStruct((d, dff), BF16), jax.ShapeDtypeStruct((d, dff), BF16),
                      jax.ShapeDtypeStruct((dff, d), BF16)]
    out = pl.pallas_call(
        functools.partial(_ffn_kernel, emit=emit),
        grid=(m // tm, nf),
        in_specs=[pl.BlockSpec((tm, d), lambda i, f: (i, 0)), pl.BlockSpec((1, d), lambda i, f: (0, 0))] + w_specs,
        out_specs=out_specs,
        out_shape=out_shape,
        scratch_shapes=[pltpu.VMEM((tm, d), BF16), pltpu.VMEM((tm, d), F32)],
        compiler_params=_cparams("parallel", "arbitrary"),
        name="ffn_cast" if emit else "ffn",
    )(h, g.reshape(1, d), wa, wb, wo)
    return out if emit else out[0]


def _ple_kernel(h_ref, p_ref, g_ref, wp_ref, wg_ref, o_ref):
    h = h_ref[...]
    xn = _rms(h, g_ref[...]).astype(BF16)
    gate = jax.nn.sigmoid(_dot(xn, wg_ref[...]))
    proj = _dot(p_ref[...].astype(BF16), wp_ref[...])
    o_ref[...] = h + proj * gate


def _ple(h, p, g, w_proj, w_gate, layer):
    m, d = h.shape
    pd = p.shape[1]
    tm = _pick(m, (256, 128, 64, 32, 16, 8))
    return pl.pallas_call(
        _ple_kernel,
        grid=(m // tm,),
        in_specs=[
            pl.BlockSpec((tm, d), lambda i: (i, 0)),
            pl.BlockSpec((tm, pd), lambda i: (i, 0)),
            pl.BlockSpec((1, d), lambda i: (0, 0)),
            pl.BlockSpec((None, pd, d), lambda i: (layer, 0, 0)),
            pl.BlockSpec((None, d, d), lambda i: (layer, 0, 0)),
        ],
        out_specs=pl.BlockSpec((tm, d), lambda i: (i, 0)),
        out_shape=jax.ShapeDtypeStruct((m, d), F32),
        compiler_params=_cparams("parallel"),
        name="ple",
    )(h, p, g.reshape(1, d), w_proj, w_gate)


def _weight_spec(block, index, layer):
    if layer is None:
        return pl.BlockSpec(block, index)
    return pl.BlockSpec((None,) + block, lambda *a: (layer,) + index(*a))


def _inproj_kernel(h_ref, g_ref, w_ref, wx_ref, y_ref, yx_ref, xn_ref):
    @pl.when(pl.program_id(1) == 0)
    def _():
        xn_ref[...] = _rms(h_ref[...], g_ref[...]).astype(BF16)
        yx_ref[...] = _dot(xn_ref[...], wx_ref[...])

    y_ref[...] = _dot(xn_ref[...], w_ref[...])


def _ssd_inproj(h, g, w, layer, n, w_dt):
    m, d = h.shape
    nx = w_dt.shape[1]
    tm = _pick(m, BIG_ROW_TILES)
    tn = _pick(n, (512, 256, 128) if tm > 512 else (1024, 512, 256, 128))
    return pl.pallas_call(
        _inproj_kernel,
        grid=(m // tm, n // tn),
        in_specs=[
            pl.BlockSpec((tm, d), lambda i, j: (i, 0)),
            pl.BlockSpec((1, d), lambda i, j: (0, 0)),
            _weight_spec((d, tn), lambda i, j: (0, j), layer),
            pl.BlockSpec((d, nx), lambda i, j: (0, 0)),
        ],
        out_specs=[pl.BlockSpec((tm, tn), lambda i, j: (i, j)), pl.BlockSpec((tm, nx), lambda i, j: (i, 0))],
        out_shape=[jax.ShapeDtypeStruct((m, n), F32), jax.ShapeDtypeStruct((m, nx), F32)],
        scratch_shapes=[pltpu.VMEM((tm, d), BF16)],
        compiler_params=_cparams("parallel", "arbitrary"),
        name="ssd_inproj",
    )(h, g.reshape(1, d), w, w_dt)


def _seg_meansq(y, seg_ref, hd):
    y2 = y * y
    hi = y2.astype(BF16)
    lo = (y2 - hi.astype(F32)).astype(BF16)
    seg = seg_ref[...]
    return (_dot(hi, seg) + _dot(lo, seg)) * (1.0 / hd)


def _qkv_kernel(*refs, hd, nsec, vt_rows, n_prev, emit):
    h_ref, g_ref, w_ref, gq_ref, gk_ref, seg_ref = refs[:6]
    outs = refs[6 + n_prev:-1]
    q1_ref, q2_ref, kf_ref, kb_ref, vf_ref = outs[:5]
    xn_ref = refs[-1]
    j = pl.program_id(1)

    @pl.when(j == 0)
    def _():
        xn_ref[...] = _rms(h_ref[...], g_ref[...]).astype(BF16)

    w = w_ref[...].astype(BF16)
    if emit:
        outs[-1][...] = w
    y = _dot(xn_ref[...], w)
    cw = seg_ref.shape[0]

    def head_norm(gain_ref):
        parts = []
        for c in range(y.shape[1] // cw):
            yc = y[:, c * cw:(c + 1) * cw]
            parts.append(yc * lax.rsqrt(_seg_meansq(yc, seg_ref, hd) + NORM_EPS))
        yn = parts[0] if len(parts) == 1 else jnp.concatenate(parts, axis=1)
        return yn * gain_ref[...]

    @pl.when(j < nsec)
    def _():
        yn = head_norm(gq_ref) * (LOG2E / math.sqrt(hd))
        lane = lax.broadcasted_iota(jnp.int32, yn.shape, 1)
        first = (lane % (2 * hd)) < hd
        q1_ref[...] = jnp.where(first, yn, 0.0).astype(BF16)
        q2_ref[...] = jnp.where(first, 0.0, yn).astype(BF16)

    @pl.when((j >= nsec) & (j < 2 * nsec))
    def _():
        yn = head_norm(gk_ref)
        kf_ref[...] = yn
        kb_ref[...] = yn.astype(BF16)

    @pl.when(j >= 2 * nsec)
    def _():
        vf_ref[...] = y
        if vt_rows:
            for r in range(y.shape[0] // vt_rows):
                outs[5][r] = y[r * vt_rows:(r + 1) * vt_rows, :].T.astype(BF16)


def _qkv(h, g, w, n, qk_gain, hd, layer=None, vt_rows=None, stack=None):
    m, d = h.shape
    emit = layer is not None
    tm = _pick(m, BIG_ROW_TILES)
    tn = _pick(n, (512, 256) if tm > 512 else (QKV_COL_TILE, 512, 256))
    cw = min(tn, 256)
    nsec = n // tn
    assert (not emit or m == tm) and (not vt_rows or tm % vt_rows == 0)

    def sec(j, s):
        return jnp.clip(j - s * nsec, 0, nsec - 1)

    lane = jnp.arange(cw)
    seg = (lane[:, None] // hd == lane[None, :] // hd).astype(BF16)
    gains = [jnp.tile(qk_gain[s].reshape(1, 2 * hd).astype(F32), (1, tn // (2 * hd))) for s in range(2)]
    in_specs = [
        pl.BlockSpec((tm, d), lambda i, j: (i, 0)),
        pl.BlockSpec((1, d), lambda i, j: (0, 0)),
        _weight_spec((d, tn), lambda i, j: (0, j), layer),
        pl.BlockSpec((1, tn), lambda i, j: (0, 0)),
        pl.BlockSpec((1, tn), lambda i, j: (0, 0)),
        pl.BlockSpec((cw, cw), lambda i, j: (0, 0)),
    ]
    args = [h, g.reshape(1, d), w, gains[0], gains[1], seg]

    def tile_spec(s):
        return pl.BlockSpec((tm, tn), lambda i, j: (i, sec(j, s)))

    aliases = {}
    if stack is not None:
        slot, slots, prev = stack

        def f32_spec(s):
            return pl.BlockSpec((None, tm, tn), lambda i, j: (slot, i, sec(j, s)))

        f32_shape = jax.ShapeDtypeStruct((slots, m, n), F32)
        if prev is not None:
            aliases = {len(args): 2, len(args) + 1: 4}
            in_specs += [pl.BlockSpec(memory_space=pl.ANY)] * 2
            args += list(prev)
    else:
        f32_spec, f32_shape = tile_spec, jax.ShapeDtypeStruct((m, n), F32)
    out_specs = [tile_spec(0), tile_spec(0), f32_spec(1), tile_spec(1), f32_spec(2)]
    out_shape = [jax.ShapeDtypeStruct((m, n), BF16)] * 2 + [f32_shape, jax.ShapeDtypeStruct((m, n), BF16),
                                                            f32_shape]
    if vt_rows:
        out_specs.append(pl.BlockSpec((tm // vt_rows, tn, vt_rows), lambda i, j: (i, sec(j, 2), 0)))
        out_shape.append(jax.ShapeDtypeStruct((m // vt_rows, n, vt_rows), BF16))
    if emit:
        out_specs.append(pl.BlockSpec((d, tn), lambda i, j: (0, j)))
        out_shape.append(jax.ShapeDtypeStruct((d, 3 * n), BF16))
    return pl.pallas_call(
        functools.partial(_qkv_kernel, hd=hd, nsec=nsec, vt_rows=vt_rows, n_prev=len(aliases), emit=emit),
        grid=(m // tm, 3 * nsec),
        in_specs=in_specs,
        out_specs=out_specs,
        out_shape=out_shape,
        scratch_shapes=[pltpu.VMEM((tm, d), BF16)],
        input_output_aliases=aliases,
        compiler_params=_cparams("parallel", "arbitrary"),
        name="qkv_proj_cast" if emit else "qkv_proj",
    )(*args)


def _mm_res_kernel(h_ref, x_ref, w_ref, o_ref, *rest, emit):
    w = w_ref[...].astype(BF16)
    if emit:
        rest[0][...] = w
    o_ref[...] = h_ref[...] + _dot(x_ref[...].astype(BF16), w)


def _matmul_residual(h, x, w, layer=None):
    m, n = h.shape
    k = x.shape[1]
    emit = layer is not None
    tm = _pick(m, BIG_ROW_TILES)
    tn = _pick(n, (512, 256, 128))
    assert not emit or m == tm
    out_specs = [pl.BlockSpec((tm, tn), lambda i, j: (i, j))]
    out_shape = [jax.ShapeDtypeStruct((m, n), F32)]
    if emit:
        out_specs.append(pl.BlockSpec((k, tn), lambda i, j: (0, j)))
        out_shape.append(jax.ShapeDtypeStruct((k, n), BF16))
    out = pl.pallas_call(
        functools.partial(_mm_res_kernel, emit=emit),
        grid=(m // tm, n // tn),
        in_specs=[
            pl.BlockSpec((tm, tn), lambda i, j: (i, j)),
            pl.BlockSpec((tm, k), lambda i, j: (i, 0)),
            _weight_spec((k, tn), lambda i, j: (0, j), layer),
        ],
        out_specs=out_specs,
        out_shape=out_shape,
        compiler_params=_cparams("parallel", "arbitrary"),
        name="matmul_residual_cast" if emit else "matmul_residual",
    )(h, x, w)
    return out if emit else out[0]


def _ssd_kernel(*refs, lc, valid, has_init, hpg, pdim, gps):
    (z_ref, x_ref, b_ref, c_ref, dt_ref) = refs[:5]
    pos = 5
    if has_init:
        tx_ref, tb_ref, tc_ref, h0_ref = refs[pos:pos + 4]
        pos += 4
    (cwx_ref, cwb_ref, cwc_ref, cbx_ref, cbb_ref, cbc_ref,
     dtb_ref, alog_ref, dsk_ref, ng_ref, hx_ref) = refs[pos:pos + 11]
    pos += 11
    y_ref, hout_ref = refs[pos:pos + 2]
    ht_ref, ex_ref, eb_ref, ec_ref = refs[pos + 2:]

    c = pl.program_id(2)
    w = hpg * pdim
    nst = ht_ref.shape[0]
    t0 = CONV_TAIL

    @pl.when(c == 0)
    def _():
        if has_init:
            ex_ref[0:t0, :] = tx_ref[0]
            eb_ref[0:t0, :] = tb_ref[0]
            ec_ref[0:t0, :] = tc_ref[0]
            ht_ref[...] = h0_ref[0].reshape(gps * w, nst).T
        else:
            ex_ref[0:t0, :] = jnp.zeros((t0, gps * w), F32)
            eb_ref[0:t0, :] = jnp.zeros((t0, gps * nst), F32)
            ec_ref[0:t0, :] = jnp.zeros((t0, gps * nst), F32)
            ht_ref[...] = jnp.zeros_like(ht_ref)

    def conv_act(e_ref, raw_ref, cw_ref, cb_ref, lanes):
        e_ref[t0:t0 + lc, lanes] = raw_ref[:, lanes]
        kw = cw_ref.shape[0]
        acc = cb_ref[:, lanes] + cw_ref[kw - 1:kw, lanes] * e_ref[t0:t0 + lc, lanes]
        for k in range(kw - 1):
            off = t0 - (kw - 1) + k
            acc = acc + cw_ref[k:k + 1, lanes] * e_ref[off:off + lc, lanes]
        e_ref[0:t0, lanes] = e_ref[lc:lc + t0, lanes]
        return _silu(acc)

    row = lax.broadcasted_iota(jnp.int32, (lc, lc), 0)
    col = lax.broadcasted_iota(jnp.int32, (lc, lc), 1)
    causal = row >= col
    tri = causal.astype(BF16)
    lane = lax.broadcasted_iota(jnp.int32, (lc, LANES), 1)
    first_half = lane < pdim

    def one_group(gi):
        wl = slice(gi * w, (gi + 1) * w)
        nl = slice(gi * nst, (gi + 1) * nst)
        hl = slice(gi * LANES, (gi + 1) * LANES)
        xc = conv_act(ex_ref, x_ref, cwx_ref, cbx_ref, wl)
        bc = conv_act(eb_ref, b_ref, cwb_ref, cbb_ref, nl)
        cc = conv_act(ec_ref, c_ref, cwc_ref, cbc_ref, nl)

        dtr = dt_ref[:, hl] + dtb_ref[:, hl]
        dt = jnp.maximum(dtr, 0.0) + jnp.log1p(jnp.exp(-jnp.abs(dtr)))
        if valid < lc:
            dt = jnp.where(lax.broadcasted_iota(jnp.int32, (lc, LANES), 0) < valid, dt, 0.0)
        da = dt * (-jnp.exp(alog_ref[:, hl]))
        d_hi, d_mid, d_lo = _split3(da)
        cs = _dot(tri, d_hi) + _dot(tri, d_mid) + _dot(tri, d_lo)
        cs_t = cs.T
        ecs = jnp.exp(cs)
        coef_s = dt * jnp.exp(cs[lc - 1:lc, :] - cs)

        e_hi, e_mid, e_lo = _split3(ecs)
        ex = _dot(jnp.concatenate([dt.astype(BF16), coef_s.astype(BF16), e_hi, e_mid, e_lo], axis=0),
                  hx_ref[...])
        dt_x, coef_x = ex[0:lc], ex[lc:2 * lc]
        ecs_x = ex[2 * lc:3 * lc] + ex[3 * lc:4 * lc] + ex[4 * lc:5 * lc]

        bcb = bc.astype(BF16)
        ccb = cc.astype(BF16)
        cb = _dot_nt(ccb, bcb)
        xd = xc * dt_x

        def decay_mat(j):
            col_j = jnp.sum(jnp.where(lane == j, cs, 0.0), axis=-1, keepdims=True)
            return (cb * jnp.exp(jnp.where(causal, col_j - cs_t[j:j + 1, :], -jnp.inf))).astype(BF16)

        y_blks = []
        for q in range(hpg // 2):
            mcat = jnp.concatenate([decay_mat(2 * q), decay_mat(2 * q + 1)], axis=1)
            xp = xd[:, q * LANES:(q + 1) * LANES]
            rhs = jnp.concatenate([jnp.where(first_half, xp, 0.0), jnp.where(first_half, 0.0, xp)],
                                  axis=0).astype(BF16)
            y_blks.append(_dot(mcat, rhs))
        y = y_blks[0] if len(y_blks) == 1 else jnp.concatenate(y_blks, axis=1)

        ht_in = ht_ref[:, wl]
        y = y + _dot(ccb, ht_in.astype(BF16)) * ecs_x
        upd = _dot(bc.T.astype(BF16), (xc * coef_x).astype(BF16))
        ht_ref[:, wl] = ht_in * ecs_x[lc - 1:lc, :] + upd

        y = y + xc * dsk_ref[:, wl]
        y = y * _silu(z_ref[:, wl])
        y = y * lax.rsqrt(jnp.mean(y * y, axis=-1, keepdims=True) + NORM_EPS) * ng_ref[:, wl]
        y_ref[:, wl] = y.astype(BF16)

    for gi in range(gps):
        one_group(gi)

    @pl.when(c == pl.num_programs(2) - 1)
    def _():
        hout_ref[0] = ht_ref[...].T.reshape(gps * hpg, pdim, nst)


def _ssd_core(zx, dtg, bsz, n_chunks, valid, dims, prm, init=None):
    d_inner, nst, groups, heads = dims
    hpg = heads // groups
    pdim = d_inner // heads
    w = hpg * pdim
    lc = SSD_CHUNK
    assert pdim * 2 == LANES and hpg % 2 == 0 and nst == LANES
    gps = _pick(groups, (SSD_GROUPS_PER_STEP, 1))
    gw, gn, gl = gps * w, gps * nst, gps * LANES
    rows = bsz * n_chunks * lc
    xo, bo, co = d_inner // gw, 2 * d_inner // gn, (2 * d_inner + groups * nst) // gn
    cbo, cco = d_inner // gn, (d_inner + groups * nst) // gn
    head_expand = (jnp.arange(LANES)[:, None] == jnp.arange(w)[None, :] // pdim).astype(BF16)

    def rowblk(b, g, c):
        return b * n_chunks + c

    in_specs = [
        pl.BlockSpec((lc, gw), lambda b, g, c: (rowblk(b, g, c), g)),
        pl.BlockSpec((lc, gw), lambda b, g, c: (rowblk(b, g, c), xo + g)),
        pl.BlockSpec((lc, gn), lambda b, g, c: (rowblk(b, g, c), bo + g)),
        pl.BlockSpec((lc, gn), lambda b, g, c: (rowblk(b, g, c), co + g)),
        pl.BlockSpec((lc, gl), lambda b, g, c: (rowblk(b, g, c), g)),
    ]
    args = [zx, zx, zx, zx, dtg]
    if init is not None:
        tail, h0, h0_layer = init
        in_specs += [
            pl.BlockSpec((1, CONV_TAIL, gw), lambda b, g, c: (b, 0, g)),
            pl.BlockSpec((1, CONV_TAIL, gn), lambda b, g, c: (b, 0, cbo + g)),
            pl.BlockSpec((1, CONV_TAIL, gn), lambda b, g, c: (b, 0, cco + g)),
            pl.BlockSpec((None, 1, gps * hpg, pdim, nst), lambda b, g, c: (h0_layer, b, g, 0, 0)),
        ]
        args += [tail, tail, tail, h0]
    kw = prm["conv_w"].shape[0]
    in_specs += [
        pl.BlockSpec((kw, gw), lambda b, g, c: (0, g)),
        pl.BlockSpec((kw, gn), lambda b, g, c: (0, cbo + g)),
        pl.BlockSpec((kw, gn), lambda b, g, c: (0, cco + g)),
        pl.BlockSpec((1, gw), lambda b, g, c: (0, g)),
        pl.BlockSpec((1, gn), lambda b, g, c: (0, cbo + g)),
        pl.BlockSpec((1, gn), lambda b, g, c: (0, cco + g)),
        pl.BlockSpec((1, gl), lambda b, g, c: (0, g)),
        pl.BlockSpec((1, gl), lambda b, g, c: (0, g)),
        pl.BlockSpec((1, gw), lambda b, g, c: (0, g)),
        pl.BlockSpec((1, gw), lambda b, g, c: (0, g)),
        pl.BlockSpec((LANES, w), lambda b, g, c: (0, 0)),
    ]
    args += [prm["conv_w"], prm["conv_w"], prm["conv_w"], prm["conv_b"], prm["conv_b"], prm["conv_b"],
             prm["dt_bias_g"], prm["a_log_g"], prm["d_x"], prm["norm_g"], head_expand]
    y, h_last = pl.pallas_call(
        functools.partial(_ssd_kernel, lc=lc, valid=valid, has_init=init is not None, hpg=hpg, pdim=pdim,
                          gps=gps),
        grid=(bsz, groups // gps, n_chunks),
        in_specs=in_specs,
        out_specs=[
            pl.BlockSpec((lc, gw), lambda b, g, c: (rowblk(b, g, c), g)),
            pl.BlockSpec((1, gps * hpg, pdim, nst), lambda b, g, c: (b, g, 0, 0)),
        ],
        out_shape=[jax.ShapeDtypeStruct((rows, d_inner), BF16),
                   jax.ShapeDtypeStruct((bsz, heads, pdim, nst), F32)],
        scratch_shapes=[pltpu.VMEM((nst, gw), F32),
                        pltpu.VMEM((lc + CONV_TAIL, gw), F32),
                        pltpu.VMEM((lc + CONV_TAIL, gn), F32),
                        pltpu.VMEM((lc + CONV_TAIL, gn), F32)],
        compiler_params=_cparams("arbitrary", "arbitrary", "arbitrary"),
        name="ssd_scan",
    )(*args)
    return y, h_last


def _attn_kernel(q1_ref, q2_ref, k_ref, vt_ref, slope_ref, lam_ref, sn_ref, o_ref,
                 m1_ref, a1_ref, m2_ref, a2_ref, *, tq, n_q, lambda_init):
    qi = pl.program_id(2)
    hw = k_ref.shape[1]
    slope2 = slope_ref[0][:, 0:1] * LOG2E
    key_row = lax.broadcasted_iota(jnp.int32, (tq, LANES), 0).astype(F32)
    streams = ((q1_ref, m1_ref, a1_ref), (q2_ref, m2_ref, a2_ref))
    for _, m_ref, a_ref in streams:
        m_ref[...] = jnp.full_like(m_ref, -jnp.inf)
        a_ref[...] = jnp.zeros_like(a_ref)
    ones_rows = jnp.ones((a1_ref.shape[0] - hw, tq), BF16)

    def tile(ki, masked):
        k = k_ref[ki * tq:(ki + 1) * tq, :]
        vt = jnp.concatenate([vt_ref[ki], ones_rows], axis=0)
        kb = slope2 * (key_row + float(ki * tq))
        kb = jnp.concatenate([kb] * (tq // LANES), axis=1)
        scores = [_dot_nt(k, q_ref[...]) + kb for q_ref, _, _ in streams]
        if masked:
            keep = (lax.broadcasted_iota(jnp.int32, (tq, tq), 0)
                    <= lax.broadcasted_iota(jnp.int32, (tq, tq), 1))
            scores = [jnp.where(keep, s, -jnp.inf) for s in scores]
        m_olds = [m_ref[...] for _, m_ref, _ in streams]
        m_news = [jnp.maximum(m_old, _rows_reduce(s, jnp.max)) for m_old, s in zip(m_olds, scores)]
        probs = [jnp.exp2((s - m_new).astype(BF16)) for s, m_new in zip(scores, m_news)]
        pvs = [_dot(vt, p) for p in probs]
        for (_, m_ref, a_ref), m_old, m_new, pv in zip(streams, m_olds, m_news, pvs):
            a_ref[...] = jnp.exp2(m_old - m_new) * a_ref[...] + pv
            m_ref[...] = m_new

    for v in range(n_q):
        @pl.when(qi == v)
        def _(v=v):
            for ki in range(v):
                tile(ki, False)
            tile(v, True)

    lam = _lambda_full(lam_ref, lambda_init)
    ot = (a1_ref[0:hw, :] / a1_ref[hw:hw + 1, :]
          - lam * (a2_ref[0:hw, :] / a2_ref[hw:hw + 1, :]))
    ot = ot * lax.rsqrt(jnp.mean(ot * ot, axis=0, keepdims=True) + NORM_EPS)
    o_ref[...] = (ot.T * (sn_ref[...] * (1.0 - lambda_init))).astype(BF16)


def _alibi_slopes(n_heads):
    return jnp.exp2(-8.0 * jnp.arange(1, n_heads + 1, dtype=F32) / n_heads)


def _attn_prompt(q1, q2, kb, vt, bsz, seq, n_heads, lam_p, sub_norm, lambda_init):
    m, width = q1.shape
    hw = width // n_heads
    tq = vt.shape[2]
    nq = seq // tq
    assert seq % tq == 0 and tq % LANES == 0
    slopes = jnp.broadcast_to(_alibi_slopes(n_heads)[:, None, None], (n_heads, 1, LANES))
    return pl.pallas_call(
        functools.partial(_attn_kernel, tq=tq, n_q=nq, lambda_init=lambda_init),
        grid=(bsz, n_heads, nq),
        in_specs=[
            pl.BlockSpec((tq, hw), lambda b, h, i: (b * nq + i, h)),
            pl.BlockSpec((tq, hw), lambda b, h, i: (b * nq + i, h)),
            pl.BlockSpec((seq, hw), lambda b, h, i: (b, h)),
            pl.BlockSpec((nq, hw, tq), lambda b, h, i: (b, h, 0)),
            pl.BlockSpec((1, 1, LANES), lambda b, h, i: (h, 0, 0)),
            pl.BlockSpec(lam_p.shape, lambda b, h, i: (0, 0)),
            pl.BlockSpec((1, hw), lambda b, h, i: (0, 0)),
        ],
        out_specs=pl.BlockSpec((tq, hw), lambda b, h, i: (b * nq + i, h)),
        out_shape=jax.ShapeDtypeStruct((m, width), BF16),
        scratch_shapes=[pltpu.VMEM((1, tq), F32), pltpu.VMEM((hw + BF16_ROWS, tq), F32),
                        pltpu.VMEM((1, tq), F32), pltpu.VMEM((hw + BF16_ROWS, tq), F32)],
        compiler_params=_cparams("parallel", "parallel", "arbitrary"),
        name="attn_prompt",
    )(q1, q2, kb, vt, slopes, lam_p.astype(F32), sub_norm.reshape(1, hw).astype(F32))


def _attn_dec_kernel(*refs, pps, n_steps, n_tok, n_heads, lambda_init):
    q_ref = refs[1]
    kp_refs, vp_refs = refs[2:2 + pps], refs[2 + pps:2 + 2 * pps]
    (kn_ref, vn_ref, base_ref, basen_ref, slc_ref, lam_ref, sn_ref,
     o_ref, m_ref, l_ref, acc_ref) = refs[2 + 2 * pps:]
    st = pl.program_id(1)
    rph = 2 * n_tok

    @pl.when(st == 0)
    def _():
        m_ref[...] = jnp.full_like(m_ref, -jnp.inf)
        l_ref[...] = jnp.zeros_like(l_ref)
        acc_ref[...] = jnp.zeros_like(acc_ref)

    q = q_ref[0]

    def flat(x3):
        return x3.reshape(x3.shape[0] * x3.shape[1], x3.shape[2]).astype(BF16)

    def update(pages, bias_ref, shifts):
        m_old = m_ref[...]
        m_new = m_old
        scores = []
        for (k_ref3, _), shift in zip(pages, shifts):
            s = _dot_nt(q, flat(k_ref3())) + bias_ref[...]
            scores.append(s)
            m_new = jnp.maximum(m_new, jnp.max(s, axis=-1, keepdims=True) + shift)
        alpha = jnp.exp2(m_old - m_new)
        l_new = alpha * l_ref[...]
        acc = alpha * acc_ref[...]
        for s, (_, v_ref3), shift in zip(scores, pages, shifts):
            p = jnp.exp2(s - (m_new - shift))
            l_new = l_new + jnp.sum(p, axis=-1, keepdims=True)
            acc = acc + _dot(p.astype(BF16), flat(v_ref3()))
        l_ref[...] = l_new
        acc_ref[...] = acc
        m_ref[...] = m_new

    @pl.when(st < n_steps)
    def _():
        first = (st * pps).astype(F32)
        update([(lambda r=kp_refs[i]: r[0, 0], lambda r=vp_refs[i]: r[0, 0]) for i in range(pps)],
               base_ref, [slc_ref[...] * (first + float(i)) for i in range(pps)])

    @pl.when(st == n_steps)
    def _():
        update([(lambda: kn_ref[0], lambda: vn_ref[0])], basen_ref, [0.0])
        lam = _lambda_full(lam_ref, lambda_init)
        on = acc_ref[...] / l_ref[...]
        hw = on.shape[1]
        for h in range(n_heads):
            o = on[h * rph:h * rph + n_tok, :] - lam * on[h * rph + n_tok:(h + 1) * rph, :]
            o = o * lax.rsqrt(jnp.mean(o * o, axis=-1, keepdims=True) + NORM_EPS)
            o_ref[0, :, h * hw:(h + 1) * hw] = (o * (sn_ref[...] * (1.0 - lambda_init))).astype(o_ref.dtype)


def _attn_sample(q1, q2, k_new, v_new, cache_k, cache_v, layer, page_table, bsz, n_tok, n_heads,
                 lam_p, sub_norm, lambda_init):
    page, hw = cache_k.shape[2], cache_k.shape[4]
    n_pages = page_table.shape[1]
    past = n_pages * page
    rph = 2 * n_tok
    nrow = rph * n_heads
    ncol = page * n_heads
    assert n_tok <= page and rph % 8 == 0 and n_heads % 8 == 0
    q12 = jnp.stack([q1, q2]).reshape(2, bsz, n_tok, n_heads, hw)
    qs = jnp.transpose(q12, (1, 3, 0, 2, 4)).reshape(bsz, nrow, hw)
    pad = ((0, 0), (0, page - n_tok), (0, 0), (0, 0))
    kn = jnp.pad(k_new.reshape(bsz, n_tok, n_heads, hw), pad)
    vn = jnp.pad(v_new.reshape(bsz, n_tok, n_heads, hw), pad)
    r = jnp.arange(nrow)
    c = jnp.arange(ncol)
    row_head, row_tok = r // rph, (r % n_tok).astype(F32)
    col_key, col_head = (c // n_heads).astype(F32), c % n_heads
    slope2 = (_alibi_slopes(n_heads) * LOG2E)[row_head]
    same_head = row_head[:, None] == col_head[None, :]
    dist_page = past + row_tok[:, None] - col_key[None, :]
    base = jnp.where(same_head, -slope2[:, None] * dist_page, -jnp.inf)
    dist_new = row_tok[:, None] - col_key[None, :]
    ok_new = same_head & (col_key[None, :] < n_tok) & (dist_new >= 0)
    base_new = jnp.where(ok_new, -slope2[:, None] * dist_new, -jnp.inf)
    slope_page = (slope2 * page).reshape(nrow, 1)

    pps = _pick(n_pages, (DEC_PAGES_PER_STEP, 2, 1))
    n_steps = n_pages // pps

    def page_idx(i, b, p, pt):
        return (layer, pt[b, jnp.minimum(p, n_steps - 1) * pps + i], 0, 0, 0)

    def const2(b, p, pt):
        return (0, 0)

    page_specs = [pl.BlockSpec((1, 1, page, n_heads, hw), functools.partial(page_idx, i)) for i in range(pps)]
    grid_spec = pltpu.PrefetchScalarGridSpec(
        num_scalar_prefetch=1,
        grid=(bsz, n_steps + 1),
        in_specs=[
            pl.BlockSpec((1, nrow, hw), lambda b, p, pt: (b, 0, 0)),
            *page_specs,
            *page_specs,
            pl.BlockSpec((1, page, n_heads, hw), lambda b, p, pt: (b, 0, 0, 0)),
            pl.BlockSpec((1, page, n_heads, hw), lambda b, p, pt: (b, 0, 0, 0)),
            pl.BlockSpec((nrow, ncol), const2),
            pl.BlockSpec((nrow, ncol), const2),
            pl.BlockSpec((nrow, 1), const2),
            pl.BlockSpec(lam_p.shape, const2),
            pl.BlockSpec((1, hw), const2),
        ],
        out_specs=pl.BlockSpec((1, n_tok, n_heads * hw), lambda b, p, pt: (b, 0, 0)),
        scratch_shapes=[pltpu.VMEM((nrow, 1), F32), pltpu.VMEM((nrow, 1), F32), pltpu.VMEM((nrow, hw), F32)],
    )
    out = pl.pallas_call(
        functools.partial(_attn_dec_kernel, pps=pps, n_steps=n_steps, n_tok=n_tok, n_heads=n_heads,
                          lambda_init=lambda_init),
        grid_spec=grid_spec,
        out_shape=jax.ShapeDtypeStruct((bsz, n_tok, n_heads * hw), BF16),
        compiler_params=_cparams("arbitrary", "arbitrary"),
        name="attn_sample",
    )(page_table, qs, *([cache_k] * pps), *([cache_v] * pps), kn, vn, base, base_new, slope_page,
      lam_p.astype(F32), sub_norm.reshape(1, hw).astype(F32))
    return out.reshape(bsz * n_tok, n_heads * hw)


def _ssd_params(j, dims, ssd_in, conv_w, conv_b, dt_bias, a_log, d_skip, norm_g):
    d_inner, nst, groups, heads = dims
    hpg = heads // groups
    pdim = d_inner // heads
    conv_dim = conv_w.shape[2]
    d_model = ssd_in.shape[1]

    def per_group(v):
        return jnp.pad(v.astype(F32).reshape(groups, hpg), ((0, 0), (0, LANES - hpg))).reshape(1, groups * LANES)

    w_dt = lax.slice(ssd_in, (j, 0, d_inner + conv_dim), (j + 1, d_model, ssd_in.shape[2]))
    w_dt = w_dt.reshape(d_model, groups, hpg)
    w_dt = jnp.pad(w_dt, ((0, 0), (0, 0), (0, LANES - hpg))).reshape(d_model, groups * LANES)
    return dict(
        w_dt=w_dt.astype(BF16),
        conv_w=conv_w[j].astype(F32),
        conv_b=conv_b[j].reshape(1, conv_dim).astype(F32),
        dt_bias_g=per_group(dt_bias[j]),
        a_log_g=per_group(a_log[j]),
        d_x=jnp.repeat(d_skip[j].astype(F32), pdim).reshape(1, d_inner),
        norm_g=norm_g[j].reshape(1, d_inner).astype(F32),
    )


def kernel(x_prompt, x_sample, cache_k, cache_v, state_conv, state_ssm, page_table, p_prompt, p_sample, norm_ffa, ffa_in, ffa_out, norm_mix, norm_ffb, ffb_in, ffb_out, norm_ple, ple_proj, ple_gate, ssd_in, ssd_conv_w, ssd_conv_b, ssd_dt_bias, ssd_a_log, ssd_d, ssd_norm, ssd_out, attn_qkv, attn_qk_norm, attn_lambda, attn_sub_norm, attn_out):
    bp, seq, d_model = x_prompt.shape
    bs, n_tok, _ = x_sample.shape
    depth = norm_ffa.shape[0]
    n_mixers = 2
    n_attn = depth // n_mixers
    heads = ssd_dt_bias.shape[1]
    d_inner = ssd_out.shape[1]
    nst = state_ssm.shape[-1]
    conv_dim = ssd_conv_w.shape[2]
    kw = ssd_conv_w.shape[1]
    groups = (conv_dim - d_inner) // (2 * nst)
    dims = (d_inner, nst, groups, heads)
    n_heads = cache_k.shape[3]
    hw = cache_k.shape[4]
    hd = hw // 2
    att_w = n_heads * hw
    lc = SSD_CHUNK
    assert seq % lc == 0 and n_tok <= lc and n_tok >= kw - 1 and kw - 1 <= CONV_TAIL

    n_zx = d_inner + conv_dim
    ple_proj_b, ple_gate_b = ple_proj.astype(BF16), ple_gate.astype(BF16)
    ssd_in_b = ssd_in.astype(BF16)
    state_ssm = state_ssm.astype(F32)

    def ffn_both(hs, g, w_in, w_out, layer):
        h_s, wa, wb, wo = _ffn(hs[1], g, w_in, w_in, w_out, layer=layer)
        return [_ffn(hs[0], g, wa, wb, wo), h_s]

    h = [x_prompt.reshape(bp * seq, d_model), x_sample.reshape(bs * n_tok, d_model)]
    ple_in = [p_prompt.reshape(depth, bp * seq, -1), p_sample.reshape(depth, bs * n_tok, -1)]
    kp_stack = vp_stack = kv_prev = None
    ks_l, vs_l = [], []
    conv_out, ssm_out = [[], []], [[], []]

    for i in range(depth):
        j = i // n_mixers
        h = ffn_both(h, norm_ffa[i], ffa_in, ffa_out, i)
        if i % n_mixers == 0:
            prm = _ssd_params(j, dims, ssd_in_b, ssd_conv_w, ssd_conv_b, ssd_dt_bias, ssd_a_log, ssd_d, ssd_norm)
            zs, dts = _ssd_inproj(h[1], norm_mix[i], ssd_in_b, j, n_zx, prm["w_dt"])
            zs3 = zs.reshape(bs, n_tok, -1)
            pad_rows = ((0, 0), (0, lc - n_tok), (0, 0))
            zs_pad = jnp.pad(zs3, pad_rows).reshape(bs * lc, -1)
            dts_pad = jnp.pad(dts.reshape(bs, n_tok, -1), pad_rows).reshape(bs * lc, -1)
            tail = jnp.pad(state_conv[j].astype(F32), ((0, 0), (CONV_TAIL - (kw - 1), 0), (0, 0)))
            y, h_last = _ssd_core(zs_pad, dts_pad, bs, 1, n_tok, dims, prm, init=(tail, state_ssm, j))
            xbc = zs3[:, :, d_inner:]
            conv_out[1].append(jnp.concatenate([state_conv[j].astype(F32), xbc], axis=1)[:, n_tok:])
            ssm_out[1].append(h_last)
            y = y.reshape(bs, lc, d_inner)[:, :n_tok].reshape(bs * n_tok, d_inner)
            h[1], w_out_b = _matmul_residual(h[1], y, ssd_out, layer=j)
            zx, dtg = _ssd_inproj(h[0], norm_mix[i], ssd_in_b, j, n_zx, prm["w_dt"])
            y, h_last = _ssd_core(zx, dtg, bp, seq // lc, lc, dims, prm)
            conv_out[0].append(zx.reshape(bp, seq, -1)[:, seq - (kw - 1):, d_inner:])
            ssm_out[0].append(h_last)
            h[0] = _matmul_residual(h[0], y, w_out_b)
        else:
            lambda_init = 0.8 - 0.6 * math.exp(-0.3 * i)
            qk_gain = attn_qk_norm[j]
            q1, q2, kf, _, vf, w_qkv_b = _qkv(h[1], norm_mix[i], attn_qkv, att_w, qk_gain, hd, layer=j)
            o = _attn_sample(q1, q2, kf, vf, cache_k, cache_v, j, page_table, bs, n_tok, n_heads,
                             attn_lambda[j], attn_sub_norm[j], lambda_init)
            ks_l.append(kf.reshape(bs, n_tok, n_heads, hw))
            vs_l.append(vf.reshape(bs, n_tok, n_heads, hw))
            h[1], w_out_b = _matmul_residual(h[1], o, attn_out, layer=j)
            q1, q2, kp_stack, kb, vp_stack, vt = _qkv(
                h[0], norm_mix[i], w_qkv_b, att_w, qk_gain, hd, vt_rows=_pick(seq, (512, 256, 128)),
                stack=(j, n_attn, kv_prev))
            kv_prev = (kp_stack, vp_stack)
            o = _attn_prompt(q1, q2, kb, vt, bp, seq, n_heads, attn_lambda[j], attn_sub_norm[j], lambda_init)
            h[0] = _matmul_residual(h[0], o, w_out_b)
        h = ffn_both(h, norm_ffb[i], ffb_in, ffb_out, i)
        h = [_ple(x, ple_in[gi][i], norm_ple[i], ple_proj_b, ple_gate_b, i) for gi, x in enumerate(h)]

    return (h[0].reshape(bp, seq, d_model), h[1].reshape(bs, n_tok, d_model),
            kp_stack.reshape(n_attn, bp, seq, n_heads, hw), vp_stack.reshape(n_attn, bp, seq, n_heads, hw),
            jnp.stack(ks_l), jnp.stack(vs_l),
            jnp.stack(conv_out[0]), jnp.stack(ssm_out[0]), jnp.stack(conv_out[1]), jnp.stack(ssm_out[1]))
```

```python
import functools
import math

import jax
import jax.numpy as jnp
from jax import lax
from jax.experimental import pallas as pl
from jax.experimental.pallas import tpu as pltpu

F32 = jnp.float32
BF16 = jnp.bfloat16
NORM_EPS = 1e-6
LOG2E = 1.4426950408889634
SSD_CHUNK = 128
LANES = 128
BF16_ROWS = 16
CONV_TAIL = 8
DEC_PAGES_PER_STEP = 4
SSD_GROUPS_PER_STEP = 4
ATTN_HEADS_PER_STEP = 2
QKV_COL_TILE = 1024
VMEM_LIMIT_BYTES = 56 * 1024 * 1024
ROW_TILES = (512, 256, 128, 64, 32, 16, 8)
BIG_ROW_TILES = (1024,) + ROW_TILES


def _cparams(*sem):
    return pltpu.CompilerParams(dimension_semantics=sem, vmem_limit_bytes=VMEM_LIMIT_BYTES)


def _pick(n, prefs):
    for t in prefs:
        if n % t == 0:
            return t
    return n


def _silu(x):
    return x * jax.nn.sigmoid(x)


def _rms(x, g):
    return x * lax.rsqrt(jnp.mean(x * x, axis=-1, keepdims=True) + NORM_EPS) * g


def _dot(a, b):
    return jnp.dot(a, b, preferred_element_type=F32)


def _dot_nt(a, b):
    return lax.dot_general(a, b, (((1,), (1,)), ((), ())), preferred_element_type=F32)


def _split3(x):
    hi = x.astype(BF16)
    r = x - hi.astype(F32)
    mid = r.astype(BF16)
    lo = (r - mid.astype(F32)).astype(BF16)
    return hi, mid, lo


def _rows_reduce(x, op):
    rows = x.shape[0]
    slabs = 8 if rows % 64 == 0 else 1
    part = op(x.reshape(slabs, rows // slabs, x.shape[1]), axis=0)
    return op(part, axis=0, keepdims=True)


def _lambda_full(lam_ref, lambda_init):
    lf = lam_ref[...]
    return (jnp.exp(jnp.sum(lf[0:1] * lf[1:2], axis=-1, keepdims=True))
            - jnp.exp(jnp.sum(lf[2:3] * lf[3:4], axis=-1, keepdims=True)) + lambda_init)


def _ffn_kernel(h_ref, g_ref, wa_ref, wb_ref, wo_ref, o_ref, *rest, emit):
    xn_ref, acc_ref = rest[-2:]
    f = pl.program_id(1)

    @pl.when(f == 0)
    def _():
        xn_ref[...] = _rms(h_ref[...], g_ref[...]).astype(BF16)
        acc_ref[...] = jnp.zeros_like(acc_ref)

    wa, wb, wo = wa_ref[...].astype(BF16), wb_ref[...].astype(BF16), wo_ref[...].astype(BF16)
    if emit:
        rest[0][...], rest[1][...], rest[2][...] = wa, wb, wo
    xn = xn_ref[...]
    a = _dot(xn, wa)
    b = _dot(xn, wb)
    acc_ref[...] += _dot((_silu(a) * b).astype(BF16), wo)

    @pl.when(f == pl.num_programs(1) - 1)
    def _():
        o_ref[...] = h_ref[...] + 0.5 * acc_ref[...]


def _ffn(h, g, wa, wb, wo, layer=None):
    m, d = h.shape
    emit = layer is not None
    dff = wo.shape[-2]
    tm = _pick(m, ROW_TILES)
    tf = _pick(dff, (512, 256, 128))
    nf = dff // tf
    if emit:
        assert m == tm
        w_specs = [pl.BlockSpec((None, d, tf), lambda i, f: (layer, 0, f)),
                   pl.BlockSpec((None, d, tf), lambda i, f: (layer, 0, f + nf)),
                   pl.BlockSpec((None, tf, d), lambda i, f: (layer, f, 0))]
    else:
        w_specs = [pl.BlockSpec((d, tf), lambda i, f: (0, f)),
                   pl.BlockSpec((d, tf), lambda i, f: (0, f)),
                   pl.BlockSpec((tf, d), lambda i, f: (f, 0))]
    out_specs = [pl.BlockSpec((tm, d), lambda i, f: (i, 0))]
    out_shape = [jax.ShapeDtypeStruct((m, d), F32)]
    if emit:
        out_specs += [pl.BlockSpec((d, tf), lambda i, f: (0, f)), pl.BlockSpec((d, tf), lambda i, f: (0, f)),
                      pl.BlockSpec((tf, d), lambda i, f: (f, 0))]
        out_shape += [jax.ShapeDtypeStruct((d, dff), BF16), jax.ShapeDtypeStruct((d, dff), BF16),
                      jax.ShapeDtypeStruct((dff, d), BF16)]
    out = pl.pallas_call(
        functools.partial(_ffn_kernel, emit=emit),
        grid=(m // tm, nf),
        in_specs=[pl.BlockSpec((tm, d), lambda i, f: (i, 0)), pl.BlockSpec((1, d), lambda i, f: (0, 0))] + w_specs,
        out_specs=out_specs,
        out_shape=out_shape,
        scratch_shapes=[pltpu.VMEM((tm, d), BF16), pltpu.VMEM((tm, d), F32)],
        compiler_params=_cparams("parallel", "arbitrary"),
        name="ffn_cast" if emit else "ffn",
    )(h, g.reshape(1, d), wa, wb, wo)
    return out if emit else out[0]


def _ple_kernel(h_ref, p_ref, g_ref, wp_ref, wg_ref, o_ref):
    h = h_ref[...]
    xn = _rms(h, g_ref[...]).astype(BF16)
    gate = jax.nn.sigmoid(_dot(xn, wg_ref[...]))
    proj = _dot(p_ref[...].astype(BF16), wp_ref[...])
    o_ref[...] = h + proj * gate


def _ple(h, p, g, w_proj, w_gate, layer):
    m, d = h.shape
    pd = p.shape[1]
    tm = _pick(m, (256, 128, 64, 32, 16, 8))
    return pl.pallas_call(
        _ple_kernel,
        grid=(m // tm,),
        in_specs=[
            pl.BlockSpec((tm, d), lambda i: (i, 0)),
            pl.BlockSpec((tm, pd), lambda i: (i, 0)),
            pl.BlockSpec((1, d), lambda i: (0, 0)),
            pl.BlockSpec((None, pd, d), lambda i: (layer, 0, 0)),
            pl.BlockSpec((None, d, d), lambda i: (layer, 0, 0)),
        ],
        out_specs=pl.BlockSpec((tm, d), lambda i: (i, 0)),
        out_shape=jax.ShapeDtypeStruct((m, d), F32),
        compiler_params=_cparams("parallel"),
        name="ple",
    )(h, p, g.reshape(1, d), w_proj, w_gate)


def _weight_spec(block, index, layer):
    if layer is None:
        return pl.BlockSpec(block, index)
    return pl.BlockSpec((None,) + block, lambda *a: (layer,) + index(*a))


def _inproj_kernel(h_ref, g_ref, w_ref, wx_ref, y_ref, yx_ref, xn_ref):
    @pl.when(pl.program_id(1) == 0)
    def _():
        xn_ref[...] = _rms(h_ref[...], g_ref[...]).astype(BF16)
        yx_ref[...] = _dot(xn_ref[...], wx_ref[...])

    y_ref[...] = _dot(xn_ref[...], w_ref[...])


def _ssd_inproj(h, g, w, layer, n, w_dt):
    m, d = h.shape
    nx = w_dt.shape[1]
    tm = _pick(m, BIG_ROW_TILES)
    tn = _pick(n, (512, 256, 128) if tm > 512 else (1024, 512, 256, 128))
    return pl.pallas_call(
        _inproj_kernel,
        grid=(m // tm, n // tn),
        in_specs=[
            pl.BlockSpec((tm, d), lambda i, j: (i, 0)),
            pl.BlockSpec((1, d), lambda i, j: (0, 0)),
            _weight_spec((d, tn), lambda i, j: (0, j), layer),
            pl.BlockSpec((d, nx), lambda i, j: (0, 0)),
        ],
        out_specs=[pl.BlockSpec((tm, tn), lambda i, j: (i, j)), pl.BlockSpec((tm, nx), lambda i, j: (i, 0))],
        out_shape=[jax.ShapeDtypeStruct((m, n), F32), jax.ShapeDtypeStruct((m, nx), F32)],
        scratch_shapes=[pltpu.VMEM((tm, d), BF16)],
        compiler_params=_cparams("parallel", "arbitrary"),
        name="ssd_inproj",
    )(h, g.reshape(1, d), w, w_dt)


def _seg_meansq(y, seg_ref, hd):
    y2 = y * y
    hi = y2.astype(BF16)
    lo = (y2 - hi.astype(F32)).astype(BF16)
    seg = seg_ref[...]
    return (_dot(hi, seg) + _dot(lo, seg)) * (1.0 / hd)


def _qkv_kernel(*refs, hd, nsec, vt_rows, n_prev, emit):
    h_ref, g_ref, w_ref, gq_ref, gk_ref, seg_ref = refs[:6]
    outs = refs[6 + n_prev:-1]
    q1_ref, q2_ref, kf_ref, kb_ref, vf_ref = outs[:5]
    xn_ref = refs[-1]
    j = pl.program_id(1)

    @pl.when(j == 0)
    def _():
        xn_ref[...] = _rms(h_ref[...], g_ref[...]).astype(BF16)

    w = w_ref[...].astype(BF16)
    if emit:
        outs[-1][...] = w
    y = _dot(xn_ref[...], w)
    cw = seg_ref.shape[0]

    def head_norm(gain_ref):
        parts = []
        for c in range(y.shape[1] // cw):
            yc = y[:, c * cw:(c + 1) * cw]
            parts.append(yc * lax.rsqrt(_seg_meansq(yc, seg_ref, hd) + NORM_EPS))
        yn = parts[0] if len(parts) == 1 else jnp.concatenate(parts, axis=1)
        return yn * gain_ref[...]

    @pl.when(j < nsec)
    def _():
        yn = head_norm(gq_ref) * (LOG2E / math.sqrt(hd))
        lane = lax.broadcasted_iota(jnp.int32, yn.shape, 1)
        first = (lane % (2 * hd)) < hd
        q1_ref[...] = jnp.where(first, yn, 0.0).astype(BF16)
        q2_ref[...] = jnp.where(first, 0.0, yn).astype(BF16)

    @pl.when((j >= nsec) & (j < 2 * nsec))
    def _():
        yn = head_norm(gk_ref)
        kf_ref[...] = yn
        kb_ref[...] = yn.astype(BF16)

    @pl.when(j >= 2 * nsec)
    def _():
        vf_ref[...] = y
        if vt_rows:
            for r in range(y.shape[0] // vt_rows):
                outs[5][r] = y[r * vt_rows:(r + 1) * vt_rows, :].T.astype(BF16)


def _qkv(h, g, w, n, qk_gain, hd, layer=None, vt_rows=None, stack=None):
    m, d = h.shape
    emit = layer is not None
    tm = _pick(m, BIG_ROW_TILES)
    tn = _pick(n, (512, 256) if tm > 512 else (QKV_COL_TILE, 512, 256))
    cw = min(tn, 256)
    nsec = n // tn
    assert (not emit or m == tm) and (not vt_rows or tm % vt_rows == 0)

    def sec(j, s):
        return jnp.clip(j - s * nsec, 0, nsec - 1)

    lane = jnp.arange(cw)
    seg = (lane[:, None] // hd == lane[None, :] // hd).astype(BF16)
    gains = [jnp.tile(qk_gain[s].reshape(1, 2 * hd).astype(F32), (1, tn // (2 * hd))) for s in range(2)]
    in_specs = [
        pl.BlockSpec((tm, d), lambda i, j: (i, 0)),
        pl.BlockSpec((1, d), lambda i, j: (0, 0)),
        _weight_spec((d, tn), lambda i, j: (0, j), layer),
        pl.BlockSpec((1, tn), lambda i, j: (0, 0)),
        pl.BlockSpec((1, tn), lambda i, j: (0, 0)),
        pl.BlockSpec((cw, cw), lambda i, j: (0, 0)),
    ]
    args = [h, g.reshape(1, d), w, gains[0], gains[1], seg]

    def tile_spec(s):
        return pl.BlockSpec((tm, tn), lambda i, j: (i, sec(j, s)))

    aliases = {}
    if stack is not None:
        slot, slots, prev = stack

        def f32_spec(s):
            return pl.BlockSpec((None, tm, tn), lambda i, j: (slot, i, sec(j, s)))

        f32_shape = jax.ShapeDtypeStruct((slots, m, n), F32)
        if prev is not None:
            aliases = {len(args): 2, len(args) + 1: 4}
            in_specs += [pl.BlockSpec(memory_space=pl.ANY)] * 2
            args += list(prev)
    else:
        f32_spec, f32_shape = tile_spec, jax.ShapeDtypeStruct((m, n), F32)
    out_specs = [tile_spec(0), tile_spec(0), f32_spec(1), tile_spec(1), f32_spec(2)]
    out_shape = [jax.ShapeDtypeStruct((m, n), BF16)] * 2 + [f32_shape, jax.ShapeDtypeStruct((m, n), BF16),
                                                            f32_shape]
    if vt_rows:
        out_specs.append(pl.BlockSpec((tm // vt_rows, tn, vt_rows), lambda i, j: (i, sec(j, 2), 0)))
        out_shape.append(jax.ShapeDtypeStruct((m // vt_rows, n, vt_rows), BF16))
    if emit:
        out_specs.append(pl.BlockSpec((d, tn), lambda i, j: (0, j)))
        out_shape.append(jax.ShapeDtypeStruct((d, 3 * n), BF16))
    return pl.pallas_call(
        functools.partial(_qkv_kernel, hd=hd, nsec=nsec, vt_rows=vt_rows, n_prev=len(aliases), emit=emit),
        grid=(m // tm, 3 * nsec),
        in_specs=in_specs,
        out_specs=out_specs,
        out_shape=out_shape,
        scratch_shapes=[pltpu.VMEM((tm, d), BF16)],
        input_output_aliases=aliases,
        compiler_params=_cparams("parallel", "arbitrary"),
        name="qkv_proj_cast" if emit else "qkv_proj",
    )(*args)


def _mm_res_kernel(h_ref, x_ref, w_ref, o_ref, *rest, emit):
    w = w_ref[...].astype(BF16)
    if emit:
        rest[0][...] = w
    o_ref[...] = h_ref[...] + _dot(x_ref[...].astype(BF16), w)


def _matmul_residual(h, x, w, layer=None):
    m, n = h.shape
    k = x.shape[1]
    emit = layer is not None
    tm = _pick(m, BIG_ROW_TILES)
    tn = _pick(n, (512, 256, 128))
    assert not emit or m == tm
    out_specs = [pl.BlockSpec((tm, tn), lambda i, j: (i, j))]
    out_shape = [jax.ShapeDtypeStruct((m, n), F32)]
    if emit:
        out_specs.append(pl.BlockSpec((k, tn), lambda i, j: (0, j)))
        out_shape.append(jax.ShapeDtypeStruct((k, n), BF16))
    out = pl.pallas_call(
        functools.partial(_mm_res_kernel, emit=emit),
        grid=(m // tm, n // tn),
        in_specs=[
            pl.BlockSpec((tm, tn), lambda i, j: (i, j)),
            pl.BlockSpec((tm, k), lambda i, j: (i, 0)),
            _weight_spec((k, tn), lambda i, j: (0, j), layer),
        ],
        out_specs=out_specs,
        out_shape=out_shape,
        compiler_params=_cparams("parallel", "arbitrary"),
        name="matmul_residual_cast" if emit else "matmul_residual",
    )(h, x, w)
    return out if emit else out[0]


def _ssd_kernel(*refs, lc, valid, has_init, hpg, pdim, gps):
    (z_ref, x_ref, b_ref, c_ref, dt_ref) = refs[:5]
    pos = 5
    if has_init:
        tx_ref, tb_ref, tc_ref, h0_ref = refs[pos:pos + 4]
        pos += 4
    (cwx_ref, cwb_ref, cwc_ref, cbx_ref, cbb_ref, cbc_ref,
     dtb_ref, alog_ref, dsk_ref, ng_ref, hx_ref) = refs[pos:pos + 11]
    pos += 11
    y_ref, hout_ref = refs[pos:pos + 2]
    ht_ref, ex_ref, eb_ref, ec_ref = refs[pos + 2:]

    c = pl.program_id(2)
    w = hpg * pdim
    nst = ht_ref.shape[0]
    t0 = CONV_TAIL

    @pl.when(c == 0)
    def _():
        if has_init:
            ex_ref[0:t0, :] = tx_ref[0]
            eb_ref[0:t0, :] = tb_ref[0]
            ec_ref[0:t0, :] = tc_ref[0]
            ht_ref[...] = h0_ref[0].reshape(gps * w, nst).T
        else:
            ex_ref[0:t0, :] = jnp.zeros((t0, gps * w), F32)
            eb_ref[0:t0, :] = jnp.zeros((t0, gps * nst), F32)
            ec_ref[0:t0, :] = jnp.zeros((t0, gps * nst), F32)
            ht_ref[...] = jnp.zeros_like(ht_ref)

    def conv_act(e_ref, raw_ref, cw_ref, cb_ref, lanes):
        e_ref[t0:t0 + lc, lanes] = raw_ref[:, lanes]
        kw = cw_ref.shape[0]
        acc = cb_ref[:, lanes] + cw_ref[kw - 1:kw, lanes] * e_ref[t0:t0 + lc, lanes]
        for k in range(kw - 1):
            off = t0 - (kw - 1) + k
            acc = acc + cw_ref[k:k + 1, lanes] * e_ref[off:off + lc, lanes]
        e_ref[0:t0, lanes] = e_ref[lc:lc + t0, lanes]
        return _silu(acc)

    row = lax.broadcasted_iota(jnp.int32, (lc, lc), 0)
    col = lax.broadcasted_iota(jnp.int32, (lc, lc), 1)
    causal = row >= col
    tri = causal.astype(BF16)
    lane = lax.broadcasted_iota(jnp.int32, (lc, LANES), 1)
    first_half = lane < pdim

    def one_group(gi):
        wl = slice(gi * w, (gi + 1) * w)
        nl = slice(gi * nst, (gi + 1) * nst)
        hl = slice(gi * LANES, (gi + 1) * LANES)
        xc = conv_act(ex_ref, x_ref, cwx_ref, cbx_ref, wl)
        bc = conv_act(eb_ref, b_ref, cwb_ref, cbb_ref, nl)
        cc = conv_act(ec_ref, c_ref, cwc_ref, cbc_ref, nl)

        dtr = dt_ref[:, hl] + dtb_ref[:, hl]
        dt = jnp.maximum(dtr, 0.0) + jnp.log1p(jnp.exp(-jnp.abs(dtr)))
        if valid < lc:
            dt = jnp.where(lax.broadcasted_iota(jnp.int32, (lc, LANES), 0) < valid, dt, 0.0)
        da = dt * (-jnp.exp(alog_ref[:, hl]))
        d_hi, d_mid, d_lo = _split3(da)
        cs = _dot(tri, d_hi) + _dot(tri, d_mid) + _dot(tri, d_lo)
        cs_t = cs.T
        ecs = jnp.exp(cs)
        coef_s = dt * jnp.exp(cs[lc - 1:lc, :] - cs)

        e_hi, e_mid, e_lo = _split3(ecs)
        ex = _dot(jnp.concatenate([dt.astype(BF16), coef_s.astype(BF16), e_hi, e_mid, e_lo], axis=0),
                  hx_ref[...])
        dt_x, coef_x = ex[0:lc], ex[lc:2 * lc]
        ecs_x = ex[2 * lc:3 * lc] + ex[3 * lc:4 * lc] + ex[4 * lc:5 * lc]

        bcb = bc.astype(BF16)
        ccb = cc.astype(BF16)
        cb = _dot_nt(ccb, bcb)
        xd = xc * dt_x

        def decay_mat(j):
            col_j = jnp.sum(jnp.where(lane == j, cs, 0.0), axis=-1, keepdims=True)
            return (cb * jnp.exp(jnp.where(causal, col_j - cs_t[j:j + 1, :], -jnp.inf))).astype(BF16)

        y_blks = []
        for q in range(hpg // 2):
            mcat = jnp.concatenate([decay_mat(2 * q), decay_mat(2 * q + 1)], axis=1)
            xp = xd[:, q * LANES:(q + 1) * LANES]
            rhs = jnp.concatenate([jnp.where(first_half, xp, 0.0), jnp.where(first_half, 0.0, xp)],
                                  axis=0).astype(BF16)
            y_blks.append(_dot(mcat, rhs))
        y = y_blks[0] if len(y_blks) == 1 else jnp.concatenate(y_blks, axis=1)

        ht_in = ht_ref[:, wl]
        y = y + _dot(ccb, ht_in.astype(BF16)) * ecs_x
        upd = _dot(bc.T.astype(BF16), (xc * coef_x).astype(BF16))
        ht_ref[:, wl] = ht_in * ecs_x[lc - 1:lc, :] + upd

        y = y + xc * dsk_ref[:, wl]
        y = y * _silu(z_ref[:, wl])
        y = y * lax.rsqrt(jnp.mean(y * y, axis=-1, keepdims=True) + NORM_EPS) * ng_ref[:, wl]
        y_ref[:, wl] = y.astype(BF16)

    for gi in range(gps):
        one_group(gi)

    @pl.when(c == pl.num_programs(2) - 1)
    def _():
        hout_ref[0] = ht_ref[...].T.reshape(gps * hpg, pdim, nst)


def _ssd_core(zx, dtg, bsz, n_chunks, valid, dims, prm, init=None):
    d_inner, nst, groups, heads = dims
    hpg = heads // groups
    pdim = d_inner // heads
    w = hpg * pdim
    lc = SSD_CHUNK
    assert pdim * 2 == LANES and hpg % 2 == 0 and nst == LANES
    gps = _pick(groups, (SSD_GROUPS_PER_STEP, 1))
    gw, gn, gl = gps * w, gps * nst, gps * LANES
    rows = bsz * n_chunks * lc
    xo, bo, co = d_inner // gw, 2 * d_inner // gn, (2 * d_inner + groups * nst) // gn
    cbo, cco = d_inner // gn, (d_inner + groups * nst) // gn
    head_expand = (jnp.arange(LANES)[:, None] == jnp.arange(w)[None, :] // pdim).astype(BF16)

    def rowblk(b, g, c):
        return b * n_chunks + c

    in_specs = [
        pl.BlockSpec((lc, gw), lambda b, g, c: (rowblk(b, g, c), g)),
        pl.BlockSpec((lc, gw), lambda b, g, c: (rowblk(b, g, c), xo + g)),
        pl.BlockSpec((lc, gn), lambda b, g, c: (rowblk(b, g, c), bo + g)),
        pl.BlockSpec((lc, gn), lambda b, g, c: (rowblk(b, g, c), co + g)),
        pl.BlockSpec((lc, gl), lambda b, g, c: (rowblk(b, g, c), g)),
    ]
    args = [zx, zx, zx, zx, dtg]
    if init is not None:
        tail, h0, h0_layer = init
        in_specs += [
            pl.BlockSpec((1, CONV_TAIL, gw), lambda b, g, c: (b, 0, g)),
            pl.BlockSpec((1, CONV_TAIL, gn), lambda b, g, c: (b, 0, cbo + g)),
            pl.BlockSpec((1, CONV_TAIL, gn), lambda b, g, c: (b, 0, cco + g)),
            pl.BlockSpec((None, 1, gps * hpg, pdim, nst), lambda b, g, c: (h0_layer, b, g, 0, 0)),
        ]
        args += [tail, tail, tail, h0]
    kw = prm["conv_w"].shape[0]
    in_specs += [
        pl.BlockSpec((kw, gw), lambda b, g, c: (0, g)),
        pl.BlockSpec((kw, gn), lambda b, g, c: (0, cbo + g)),
        pl.BlockSpec((kw, gn), lambda b, g, c: (0, cco + g)),
        pl.BlockSpec((1, gw), lambda b, g, c: (0, g)),
        pl.BlockSpec((1, gn), lambda b, g, c: (0, cbo + g)),
        pl.BlockSpec((1, gn), lambda b, g, c: (0, cco + g)),
        pl.BlockSpec((1, gl), lambda b, g, c: (0, g)),
        pl.BlockSpec((1, gl), lambda b, g, c: (0, g)),
        pl.BlockSpec((1, gw), lambda b, g, c: (0, g)),
        pl.BlockSpec((1, gw), lambda b, g, c: (0, g)),
        pl.BlockSpec((LANES, w), lambda b, g, c: (0, 0)),
    ]
    args += [prm["conv_w"], prm["conv_w"], prm["conv_w"], prm["conv_b"], prm["conv_b"], prm["conv_b"],
             prm["dt_bias_g"], prm["a_log_g"], prm["d_x"], prm["norm_g"], head_expand]
    y, h_last = pl.pallas_call(
        functools.partial(_ssd_kernel, lc=lc, valid=valid, has_init=init is not None, hpg=hpg, pdim=pdim,
                          gps=gps),
        grid=(bsz, groups // gps, n_chunks),
        in_specs=in_specs,
        out_specs=[
            pl.BlockSpec((lc, gw), lambda b, g, c: (rowblk(b, g, c), g)),
            pl.BlockSpec((1, gps * hpg, pdim, nst), lambda b, g, c: (b, g, 0, 0)),
        ],
        out_shape=[jax.ShapeDtypeStruct((rows, d_inner), BF16),
                   jax.ShapeDtypeStruct((bsz, heads, pdim, nst), F32)],
        scratch_shapes=[pltpu.VMEM((nst, gw), F32),
                        pltpu.VMEM((lc + CONV_TAIL, gw), F32),
                        pltpu.VMEM((lc + CONV_TAIL, gn), F32),
                        pltpu.VMEM((lc + CONV_TAIL, gn), F32)],
        compiler_params=_cparams("arbitrary", "arbitrary", "arbitrary"),
        name="ssd_scan",
    )(*args)
    return y, h_last


def _attn_kernel(q1_ref, q2_ref, k_ref, vt_ref, slope_ref, lam_ref, sn_ref, o_ref, m_ref, a_ref,
                 *, tq, n_q, hw, lambda_init):
    qi = pl.program_id(2)
    hps = k_ref.shape[1] // hw
    key_row = lax.broadcasted_iota(jnp.int32, (tq, LANES), 0).astype(F32)
    m_ref[...] = jnp.full_like(m_ref, -jnp.inf)
    a_ref[...] = jnp.zeros_like(a_ref)
    ones_rows = jnp.ones((a_ref.shape[1] - hw, tq), BF16)

    def tile(e, ki, masked):
        lanes = slice(e * hw, (e + 1) * hw)
        slope2 = slope_ref[e][:, 0:1] * LOG2E
        k = k_ref[ki * tq:(ki + 1) * tq, lanes]
        vt = jnp.concatenate([vt_ref[ki, lanes, :], ones_rows], axis=0)
        kb = slope2 * (key_row + float(ki * tq))
        kb = jnp.concatenate([kb] * (tq // LANES), axis=1)
        scores = [_dot_nt(k, q_ref[:, lanes]) + kb for q_ref in (q1_ref, q2_ref)]
        if masked:
            keep = (lax.broadcasted_iota(jnp.int32, (tq, tq), 0)
                    <= lax.broadcasted_iota(jnp.int32, (tq, tq), 1))
            scores = [jnp.where(keep, s, -jnp.inf) for s in scores]
        slots = (2 * e, 2 * e + 1)
        m_olds = [m_ref[t] for t in slots]
        m_news = [jnp.maximum(m_old, _rows_reduce(s, jnp.max)) for m_old, s in zip(m_olds, scores)]
        probs = [jnp.exp2((s - m_new).astype(BF16)) for s, m_new in zip(scores, m_news)]
        pvs = [_dot(vt, p) for p in probs]
        for t, m_old, m_new, pv in zip(slots, m_olds, m_news, pvs):
            a_ref[t] = jnp.exp2(m_old - m_new) * a_ref[t] + pv
            m_ref[t] = m_new

    for v in range(n_q):
        @pl.when(qi == v)
        def _(v=v):
            for ki in range(v):
                for e in range(hps):
                    tile(e, ki, False)
            for e in range(hps):
                tile(e, v, True)

    lam = _lambda_full(lam_ref, lambda_init)
    for e in range(hps):
        a1, a2 = a_ref[2 * e], a_ref[2 * e + 1]
        ot = a1[0:hw, :] / a1[hw:hw + 1, :] - lam * (a2[0:hw, :] / a2[hw:hw + 1, :])
        ot = ot * lax.rsqrt(jnp.mean(ot * ot, axis=0, keepdims=True) + NORM_EPS)
        o_ref[:, e * hw:(e + 1) * hw] = (ot.T * (sn_ref[...] * (1.0 - lambda_init))).astype(BF16)


def _alibi_slopes(n_heads):
    return jnp.exp2(-8.0 * jnp.arange(1, n_heads + 1, dtype=F32) / n_heads)


def _attn_prompt(q1, q2, kb, vt, bsz, seq, n_heads, lam_p, sub_norm, lambda_init):
    m, width = q1.shape
    hw = width // n_heads
    tq = vt.shape[2]
    nq = seq // tq
    assert seq % tq == 0 and tq % LANES == 0
    hps = _pick(n_heads, (ATTN_HEADS_PER_STEP, 1))
    gw = hps * hw
    slopes = jnp.broadcast_to(_alibi_slopes(n_heads)[:, None, None], (n_heads, 1, LANES))
    return pl.pallas_call(
        functools.partial(_attn_kernel, tq=tq, n_q=nq, hw=hw, lambda_init=lambda_init),
        grid=(bsz, n_heads // hps, nq),
        in_specs=[
            pl.BlockSpec((tq, gw), lambda b, h, i: (b * nq + i, h)),
            pl.BlockSpec((tq, gw), lambda b, h, i: (b * nq + i, h)),
            pl.BlockSpec((seq, gw), lambda b, h, i: (b, h)),
            pl.BlockSpec((nq, gw, tq), lambda b, h, i: (b, h, 0)),
            pl.BlockSpec((hps, 1, LANES), lambda b, h, i: (h, 0, 0)),
            pl.BlockSpec(lam_p.shape, lambda b, h, i: (0, 0)),
            pl.BlockSpec((1, hw), lambda b, h, i: (0, 0)),
        ],
        out_specs=pl.BlockSpec((tq, gw), lambda b, h, i: (b * nq + i, h)),
        out_shape=jax.ShapeDtypeStruct((m, width), BF16),
        scratch_shapes=[pltpu.VMEM((2 * hps, 1, tq), F32), pltpu.VMEM((2 * hps, hw + BF16_ROWS, tq), F32)],
        compiler_params=_cparams("parallel", "parallel", "arbitrary"),
        name="attn_prompt",
    )(q1, q2, kb, vt, slopes, lam_p.astype(F32), sub_norm.reshape(1, hw).astype(F32))


def _attn_dec_kernel(*refs, pps, n_steps, n_tok, n_heads, lambda_init):
    q_ref = refs[1]
    kp_refs, vp_refs = refs[2:2 + pps], refs[2 + pps:2 + 2 * pps]
    (kn_ref, vn_ref, base_ref, basen_ref, slc_ref, lam_ref, sn_ref,
     o_ref, m_ref, l_ref, acc_ref) = refs[2 + 2 * pps:]
    st = pl.program_id(1)
    rph = 2 * n_tok

    @pl.when(st == 0)
    def _():
        m_ref[...] = jnp.full_like(m_ref, -jnp.inf)
        l_ref[...] = jnp.zeros_like(l_ref)
        acc_ref[...] = jnp.zeros_like(acc_ref)

    q = q_ref[0]

    def flat(x3):
        return x3.reshape(x3.shape[0] * x3.shape[1], x3.shape[2]).astype(BF16)

    def update(pages, bias_ref, shifts):
        m_old = m_ref[...]
        m_new = m_old
        scores = []
        for (k_ref3, _), shift in zip(pages, shifts):
            s = _dot_nt(q, flat(k_ref3())) + bias_ref[...]
            scores.append(s)
            m_new = jnp.maximum(m_new, jnp.max(s, axis=-1, keepdims=True) + shift)
        alpha = jnp.exp2(m_old - m_new)
        l_new = alpha * l_ref[...]
        acc = alpha * acc_ref[...]
        for s, (_, v_ref3), shift in zip(scores, pages, shifts):
            p = jnp.exp2(s - (m_new - shift))
            l_new = l_new + jnp.sum(p, axis=-1, keepdims=True)
            acc = acc + _dot(p.astype(BF16), flat(v_ref3()))
        l_ref[...] = l_new
        acc_ref[...] = acc
        m_ref[...] = m_new

    @pl.when(st < n_steps)
    def _():
        first = (st * pps).astype(F32)
        update([(lambda r=kp_refs[i]: r[0, 0], lambda r=vp_refs[i]: r[0, 0]) for i in range(pps)],
               base_ref, [slc_ref[...] * (first + float(i)) for i in range(pps)])

    @pl.when(st == n_steps)
    def _():
        update([(lambda: kn_ref[0], lambda: vn_ref[0])], basen_ref, [0.0])
        lam = _lambda_full(lam_ref, lambda_init)
        on = acc_ref[...] / l_ref[...]
        hw = on.shape[1]
        for h in range(n_heads):
            o = on[h * rph:h * rph + n_tok, :] - lam * on[h * rph + n_tok:(h + 1) * rph, :]
            o = o * lax.rsqrt(jnp.mean(o * o, axis=-1, keepdims=True) + NORM_EPS)
            o_ref[0, :, h * hw:(h + 1) * hw] = (o * (sn_ref[...] * (1.0 - lambda_init))).astype(o_ref.dtype)


def _attn_sample(q1, q2, k_new, v_new, cache_k, cache_v, layer, page_table, bsz, n_tok, n_heads,
                 lam_p, sub_norm, lambda_init):
    page, hw = cache_k.shape[2], cache_k.shape[4]
    n_pages = page_table.shape[1]
    past = n_pages * page
    rph = 2 * n_tok
    nrow = rph * n_heads
    ncol = page * n_heads
    assert n_tok <= page and rph % 8 == 0 and n_heads % 8 == 0
    q12 = jnp.stack([q1, q2]).reshape(2, bsz, n_tok, n_heads, hw)
    qs = jnp.transpose(q12, (1, 3, 0, 2, 4)).reshape(bsz, nrow, hw)
    pad = ((0, 0), (0, page - n_tok), (0, 0), (0, 0))
    kn = jnp.pad(k_new.reshape(bsz, n_tok, n_heads, hw), pad)
    vn = jnp.pad(v_new.reshape(bsz, n_tok, n_heads, hw), pad)
    r = jnp.arange(nrow)
    c = jnp.arange(ncol)
    row_head, row_tok = r // rph, (r % n_tok).astype(F32)
    col_key, col_head = (c // n_heads).astype(F32), c % n_heads
    slope2 = (_alibi_slopes(n_heads) * LOG2E)[row_head]
    same_head = row_head[:, None] == col_head[None, :]
    dist_page = past + row_tok[:, None] - col_key[None, :]
    base = jnp.where(same_head, -slope2[:, None] * dist_page, -jnp.inf)
    dist_new = row_tok[:, None] - col_key[None, :]
    ok_new = same_head & (col_key[None, :] < n_tok) & (dist_new >= 0)
    base_new = jnp.where(ok_new, -slope2[:, None] * dist_new, -jnp.inf)
    slope_page = (slope2 * page).reshape(nrow, 1)

    pps = _pick(n_pages, (DEC_PAGES_PER_STEP, 2, 1))
    n_steps = n_pages // pps

    def page_idx(i, b, p, pt):
        return (layer, pt[b, jnp.minimum(p, n_steps - 1) * pps + i], 0, 0, 0)

    def const2(b, p, pt):
        return (0, 0)

    page_specs = [pl.BlockSpec((1, 1, page, n_heads, hw), functools.partial(page_idx, i)) for i in range(pps)]
    grid_spec = pltpu.PrefetchScalarGridSpec(
        num_scalar_prefetch=1,
        grid=(bsz, n_steps + 1),
        in_specs=[
            pl.BlockSpec((1, nrow, hw), lambda b, p, pt: (b, 0, 0)),
            *page_specs,
            *page_specs,
            pl.BlockSpec((1, page, n_heads, hw), lambda b, p, pt: (b, 0, 0, 0)),
            pl.BlockSpec((1, page, n_heads, hw), lambda b, p, pt: (b, 0, 0, 0)),
            pl.BlockSpec((nrow, ncol), const2),
            pl.BlockSpec((nrow, ncol), const2),
            pl.BlockSpec((nrow, 1), const2),
            pl.BlockSpec(lam_p.shape, const2),
            pl.BlockSpec((1, hw), const2),
        ],
        out_specs=pl.BlockSpec((1, n_tok, n_heads * hw), lambda b, p, pt: (b, 0, 0)),
        scratch_shapes=[pltpu.VMEM((nrow, 1), F32), pltpu.VMEM((nrow, 1), F32), pltpu.VMEM((nrow, hw), F32)],
    )
    out = pl.pallas_call(
        functools.partial(_attn_dec_kernel, pps=pps, n_steps=n_steps, n_tok=n_tok, n_heads=n_heads,
                          lambda_init=lambda_init),
        grid_spec=grid_spec,
        out_shape=jax.ShapeDtypeStruct((bsz, n_tok, n_heads * hw), BF16),
        compiler_params=_cparams("arbitrary", "arbitrary"),
        name="attn_sample",
    )(page_table, qs, *([cache_k] * pps), *([cache_v] * pps), kn, vn, base, base_new, slope_page,
      lam_p.astype(F32), sub_norm.reshape(1, hw).astype(F32))
    return out.reshape(bsz * n_tok, n_heads * hw)


def _ssd_params(j, dims, ssd_in, conv_w, conv_b, dt_bias, a_log, d_skip, norm_g):
    d_inner, nst, groups, heads = dims
    hpg = heads // groups
    pdim = d_inner // heads
    conv_dim = conv_w.shape[2]
    d_model = ssd_in.shape[1]

    def per_group(v):
        return jnp.pad(v.astype(F32).reshape(groups, hpg), ((0, 0), (0, LANES - hpg))).reshape(1, groups * LANES)

    w_dt = lax.slice(ssd_in, (j, 0, d_inner + conv_dim), (j + 1, d_model, ssd_in.shape[2]))
    w_dt = w_dt.reshape(d_model, groups, hpg)
    w_dt = jnp.pad(w_dt, ((0, 0), (0, 0), (0, LANES - hpg))).reshape(d_model, groups * LANES)
    return dict(
        w_dt=w_dt.astype(BF16),
        conv_w=conv_w[j].astype(F32),
        conv_b=conv_b[j].reshape(1, conv_dim).astype(F32),
        dt_bias_g=per_group(dt_bias[j]),
        a_log_g=per_group(a_log[j]),
        d_x=jnp.repeat(d_skip[j].astype(F32), pdim).reshape(1, d_inner),
        norm_g=norm_g[j].reshape(1, d_inner).astype(F32),
    )


def kernel(x_prompt, x_sample, cache_k, cache_v, state_conv, state_ssm, page_table, p_prompt, p_sample, norm_ffa, ffa_in, ffa_out, norm_mix, norm_ffb, ffb_in, ffb_out, norm_ple, ple_proj, ple_gate, ssd_in, ssd_conv_w, ssd_conv_b, ssd_dt_bias, ssd_a_log, ssd_d, ssd_norm, ssd_out, attn_qkv, attn_qk_norm, attn_lambda, attn_sub_norm, attn_out):
    bp, seq, d_model = x_prompt.shape
    bs, n_tok, _ = x_sample.shape
    depth = norm_ffa.shape[0]
    n_mixers = 2
    n_attn = depth // n_mixers
    heads = ssd_dt_bias.shape[1]
    d_inner = ssd_out.shape[1]
    nst = state_ssm.shape[-1]
    conv_dim = ssd_conv_w.shape[2]
    kw = ssd_conv_w.shape[1]
    groups = (conv_dim - d_inner) // (2 * nst)
    dims = (d_inner, nst, groups, heads)
    n_heads = cache_k.shape[3]
    hw = cache_k.shape[4]
    hd = hw // 2
    att_w = n_heads * hw
    lc = SSD_CHUNK
    assert seq % lc == 0 and n_tok <= lc and n_tok >= kw - 1 and kw - 1 <= CONV_TAIL

    n_zx = d_inner + conv_dim
    ple_proj_b, ple_gate_b = ple_proj.astype(BF16), ple_gate.astype(BF16)
    ssd_in_b = ssd_in.astype(BF16)
    state_ssm = state_ssm.astype(F32)

    def ffn_both(hs, g, w_in, w_out, layer):
        h_s, wa, wb, wo = _ffn(hs[1], g, w_in, w_in, w_out, layer=layer)
        return [_ffn(hs[0], g, wa, wb, wo), h_s]

    h = [x_prompt.reshape(bp * seq, d_model), x_sample.reshape(bs * n_tok, d_model)]
    ple_in = [p_prompt.reshape(depth, bp * seq, -1), p_sample.reshape(depth, bs * n_tok, -1)]
    kp_stack = vp_stack = kv_prev = None
    ks_l, vs_l = [], []
    conv_out, ssm_out = [[], []], [[], []]

    for i in range(depth):
        j = i // n_mixers
        h = ffn_both(h, norm_ffa[i], ffa_in, ffa_out, i)
        if i % n_mixers == 0:
            prm = _ssd_params(j, dims, ssd_in_b, ssd_conv_w, ssd_conv_b, ssd_dt_bias, ssd_a_log, ssd_d, ssd_norm)
            zs, dts = _ssd_inproj(h[1], norm_mix[i], ssd_in_b, j, n_zx, prm["w_dt"])
            zs3 = zs.reshape(bs, n_tok, -1)
            pad_rows = ((0, 0), (0, lc - n_tok), (0, 0))
            zs_pad = jnp.pad(zs3, pad_rows).reshape(bs * lc, -1)
            dts_pad = jnp.pad(dts.reshape(bs, n_tok, -1), pad_rows).reshape(bs * lc, -1)
            tail = jnp.pad(state_conv[j].astype(F32), ((0, 0), (CONV_TAIL - (kw - 1), 0), (0, 0)))
            y, h_last = _ssd_core(zs_pad, dts_pad, bs, 1, n_tok, dims, prm, init=(tail, state_ssm, j))
            xbc = zs3[:, :, d_inner:]
            conv_out[1].append(jnp.concatenate([state_conv[j].astype(F32), xbc], axis=1)[:, n_tok:])
            ssm_out[1].append(h_last)
            y = y.reshape(bs, lc, d_inner)[:, :n_tok].reshape(bs * n_tok, d_inner)
            h[1], w_out_b = _matmul_residual(h[1], y, ssd_out, layer=j)
            zx, dtg = _ssd_inproj(h[0], norm_mix[i], ssd_in_b, j, n_zx, prm["w_dt"])
            y, h_last = _ssd_core(zx, dtg, bp, seq // lc, lc, dims, prm)
            conv_out[0].append(zx.reshape(bp, seq, -1)[:, seq - (kw - 1):, d_inner:])
            ssm_out[0].append(h_last)
            h[0] = _matmul_residual(h[0], y, w_out_b)
        else:
            lambda_init = 0.8 - 0.6 * math.exp(-0.3 * i)
            qk_gain = attn_qk_norm[j]
            q1, q2, kf, _, vf, w_qkv_b = _qkv(h[1], norm_mix[i], attn_qkv, att_w, qk_gain, hd, layer=j)
            o = _attn_sample(q1, q2, kf, vf, cache_k, cache_v, j, page_table, bs, n_tok, n_heads,
                             attn_lambda[j], attn_sub_norm[j], lambda_init)
            ks_l.append(kf.reshape(bs, n_tok, n_heads, hw))
            vs_l.append(vf.reshape(bs, n_tok, n_heads, hw))
            h[1], w_out_b = _matmul_residual(h[1], o, attn_out, layer=j)
            q1, q2, kp_stack, kb, vp_stack, vt = _qkv(
                h[0], norm_mix[i], w_qkv_b, att_w, qk_gain, hd, vt_rows=_pick(seq, (512, 256, 128)),
                stack=(j, n_attn, kv_prev))
            kv_prev = (kp_stack, vp_stack)
            o = _attn_prompt(q1, q2, kb, vt, bp, seq, n_heads, attn_lambda[j], attn_sub_norm[j], lambda_init)
            h[0] = _matmul_residual(h[0], o, w_out_b)
        h = ffn_both(h, norm_ffb[i], ffb_in, ffb_out, i)
        h = [_ple(x, ple_in[gi][i], norm_ple[i], ple_proj_b, ple_gate_b, i) for gi, x in enumerate(h)]

    return (h[0].reshape(bp, seq, d_model), h[1].reshape(bs, n_tok, d_model),
            kp_stack.reshape(n_attn, bp, seq, n_heads, hw), vp_stack.reshape(n_attn, bp, seq, n_heads, hw),
            jnp.stack(ks_l), jnp.stack(vs_l),
            jnp.stack(conv_out[0]), jnp.stack(ssm_out[0]), jnp.stack(conv_out[1]), jnp.stack(ssm_out[1]))
```

```python
import functools
import math

import jax
import jax.numpy as jnp
from jax import lax
from jax.experimental import pallas as pl
from jax.experimental.pallas import tpu as pltpu

F32 = jnp.float32
BF16 = jnp.bfloat16
NORM_EPS = 1e-6
LOG2E = 1.4426950408889634
SSD_CHUNK = 128
LANES = 128
BF16_ROWS = 16
CONV_TAIL = 8
DEC_PAGES_PER_STEP = 4
SSD_GROUPS_PER_STEP = 8
ATTN_HEADS_PER_STEP = 4
QKV_COL_TILE = 1024
VMEM_LIMIT_BYTES = 56 * 1024 * 1024
ROW_TILES = (512, 256, 128, 64, 32, 16, 8)
BIG_ROW_TILES = (1024,) + ROW_TILES


def _cparams(*sem):
    return pltpu.CompilerParams(dimension_semantics=sem, vmem_limit_bytes=VMEM_LIMIT_BYTES)


def _pick(n, prefs):
    for t in prefs:
        if n % t == 0:
            return t
    return n


def _silu(x):
    return x * jax.nn.sigmoid(x)


def _rms(x, g):
    return x * lax.rsqrt(jnp.mean(x * x, axis=-1, keepdims=True) + NORM_EPS) * g


def _dot(a, b):
    return jnp.dot(a, b, preferred_element_type=F32)


def _dot_nt(a, b):
    return lax.dot_general(a, b, (((1,), (1,)), ((), ())), preferred_element_type=F32)


def _split3(x):
    hi = x.astype(BF16)
    r = x - hi.astype(F32)
    mid = r.astype(BF16)
    lo = (r - mid.astype(F32)).astype(BF16)
    return hi, mid, lo


def _rows_reduce(x, op):
    rows = x.shape[0]
    slabs = 8 if rows % 64 == 0 else 1
    part = op(x.reshape(slabs, rows // slabs, x.shape[1]), axis=0)
    return op(part, axis=0, keepdims=True)


def _lambda_full(lam_ref, lambda_init):
    lf = lam_ref[...]
    return (jnp.exp(jnp.sum(lf[0:1] * lf[1:2], axis=-1, keepdims=True))
            - jnp.exp(jnp.sum(lf[2:3] * lf[3:4], axis=-1, keepdims=True)) + lambda_init)


def _ffn_kernel(h_ref, g_ref, wa_ref, wb_ref, wo_ref, o_ref, *rest, emit):
    xn_ref, acc_ref = rest[-2:]
    f = pl.program_id(1)

    @pl.when(f == 0)
    def _():
        xn_ref[...] = _rms(h_ref[...], g_ref[...]).astype(BF16)
        acc_ref[...] = jnp.zeros_like(acc_ref)

    wa, wb, wo = wa_ref[...].astype(BF16), wb_ref[...].astype(BF16), wo_ref[...].astype(BF16)
    if emit:
        rest[0][...], rest[1][...], rest[2][...] = wa, wb, wo
    xn = xn_ref[...]
    a = _dot(xn, wa)
    b = _dot(xn, wb)
    acc_ref[...] += _dot((_silu(a) * b).astype(BF16), wo)

    @pl.when(f == pl.num_programs(1) - 1)
    def _():
        o_ref[...] = h_ref[...] + 0.5 * acc_ref[...]


def _ffn(h, g, wa, wb, wo, layer=None):
    m, d = h.shape
    emit = layer is not None
    dff = wo.shape[-2]
    tm = _pick(m, ROW_TILES)
    tf = _pick(dff, (512, 256, 128))
    nf = dff // tf
    if emit:
        assert m == tm
        w_specs = [pl.BlockSpec((None, d, tf), lambda i, f: (layer, 0, f)),
                   pl.BlockSpec((None, d, tf), lambda i, f: (layer, 0, f + nf)),
                   pl.BlockSpec((None, tf, d), lambda i, f: (layer, f, 0))]
    else:
        w_specs = [pl.BlockSpec((d, tf), lambda i, f: (0, f)),
                   pl.BlockSpec((d, tf), lambda i, f: (0, f)),
                   pl.BlockSpec((tf, d), lambda i, f: (f, 0))]
    out_specs = [pl.BlockSpec((tm, d), lambda i, f: (i, 0))]
    out_shape = [jax.ShapeDtypeStruct((m, d), F32)]
    if emit:
        out_specs += [pl.BlockSpec((d, tf), lambda i, f: (0, f)), pl.BlockSpec((d, tf), lambda i, f: (0, f)),
                      pl.BlockSpec((tf, d), lambda i, f: (f, 0))]
        out_shape += [jax.ShapeDtypeStruct((d, dff), BF16), jax.ShapeDtypeStruct((d, dff), BF16),
                      jax.ShapeDtypeStruct((dff, d), BF16)]
    out = pl.pallas_call(
        functools.partial(_ffn_kernel, emit=emit),
        grid=(m // tm, nf),
        in_specs=[pl.BlockSpec((tm, d), lambda i, f: (i, 0)), pl.BlockSpec((1, d), lambda i, f: (0, 0))] + w_specs,
        out_specs=out_specs,
        out_shape=out_shape,
        scratch_shapes=[pltpu.VMEM((tm, d), BF16), pltpu.VMEM((tm, d), F32)],
        compiler_params=_cparams("parallel", "arbitrary"),
        name="ffn_cast" if emit else "ffn",
    )(h, g.reshape(1, d), wa, wb, wo)
    return out if emit else out[0]


def _ple_kernel(h_ref, p_ref, g_ref, wp_ref, wg_ref, o_ref):
    h = h_ref[...]
    xn = _rms(h, g_ref[...]).astype(BF16)
    gate = jax.nn.sigmoid(_dot(xn, wg_ref[...]))
    proj = _dot(p_ref[...].astype(BF16), wp_ref[...])
    o_ref[...] = h + proj * gate


def _ple(h, p, g, w_proj, w_gate, layer):
    m, d = h.shape
    pd = p.shape[1]
    tm = _pick(m, (256, 128, 64, 32, 16, 8))
    return pl.pallas_call(
        _ple_kernel,
        grid=(m // tm,),
        in_specs=[
            pl.BlockSpec((tm, d), lambda i: (i, 0)),
            pl.BlockSpec((tm, pd), lambda i: (i, 0)),
            pl.BlockSpec((1, d), lambda i: (0, 0)),
            pl.BlockSpec((None, pd, d), lambda i: (layer, 0, 0)),
            pl.BlockSpec((None, d, d), lambda i: (layer, 0, 0)),
        ],
        out_specs=pl.BlockSpec((tm, d), lambda i: (i, 0)),
        out_shape=jax.ShapeDtypeStruct((m, d), F32),
        compiler_params=_cparams("parallel"),
        name="ple",
    )(h, p, g.reshape(1, d), w_proj, w_gate)


def _weight_spec(block, index, layer):
    if layer is None:
        return pl.BlockSpec(block, index)
    return pl.BlockSpec((None,) + block, lambda *a: (layer,) + index(*a))


def _inproj_kernel(h_ref, g_ref, w_ref, wx_ref, y_ref, yx_ref, xn_ref):
    @pl.when(pl.program_id(1) == 0)
    def _():
        xn_ref[...] = _rms(h_ref[...], g_ref[...]).astype(BF16)
        yx_ref[...] = _dot(xn_ref[...], wx_ref[...])

    y_ref[...] = _dot(xn_ref[...], w_ref[...])


def _ssd_inproj(h, g, w, layer, n, w_dt):
    m, d = h.shape
    nx = w_dt.shape[1]
    tm = _pick(m, BIG_ROW_TILES)
    tn = _pick(n, (512, 256, 128) if tm > 512 else (1024, 512, 256, 128))
    return pl.pallas_call(
        _inproj_kernel,
        grid=(m // tm, n // tn),
        in_specs=[
            pl.BlockSpec((tm, d), lambda i, j: (i, 0)),
            pl.BlockSpec((1, d), lambda i, j: (0, 0)),
            _weight_spec((d, tn), lambda i, j: (0, j), layer),
            pl.BlockSpec((d, nx), lambda i, j: (0, 0)),
        ],
        out_specs=[pl.BlockSpec((tm, tn), lambda i, j: (i, j)), pl.BlockSpec((tm, nx), lambda i, j: (i, 0))],
        out_shape=[jax.ShapeDtypeStruct((m, n), F32), jax.ShapeDtypeStruct((m, nx), F32)],
        scratch_shapes=[pltpu.VMEM((tm, d), BF16)],
        compiler_params=_cparams("parallel", "arbitrary"),
        name="ssd_inproj",
    )(h, g.reshape(1, d), w, w_dt)


def _seg_meansq(y, seg_ref, hd):
    y2 = y * y
    hi = y2.astype(BF16)
    lo = (y2 - hi.astype(F32)).astype(BF16)
    seg = seg_ref[...]
    return (_dot(hi, seg) + _dot(lo, seg)) * (1.0 / hd)


def _qkv_kernel(*refs, hd, nsec, vt_rows, n_prev, emit):
    h_ref, g_ref, w_ref, gq_ref, gk_ref, seg_ref = refs[:6]
    outs = refs[6 + n_prev:-1]
    q1_ref, q2_ref, kf_ref, kb_ref, vf_ref = outs[:5]
    xn_ref = refs[-1]
    j = pl.program_id(1)

    @pl.when(j == 0)
    def _():
        xn_ref[...] = _rms(h_ref[...], g_ref[...]).astype(BF16)

    w = w_ref[...].astype(BF16)
    if emit:
        outs[-1][...] = w
    y = _dot(xn_ref[...], w)
    cw = seg_ref.shape[0]

    def head_norm(gain_ref):
        parts = []
        for c in range(y.shape[1] // cw):
            yc = y[:, c * cw:(c + 1) * cw]
            parts.append(yc * lax.rsqrt(_seg_meansq(yc, seg_ref, hd) + NORM_EPS))
        yn = parts[0] if len(parts) == 1 else jnp.concatenate(parts, axis=1)
        return yn * gain_ref[...]

    @pl.when(j < nsec)
    def _():
        yn = head_norm(gq_ref) * (LOG2E / math.sqrt(hd))
        lane = lax.broadcasted_iota(jnp.int32, yn.shape, 1)
        first = (lane % (2 * hd)) < hd
        q1_ref[...] = jnp.where(first, yn, 0.0).astype(BF16)
        q2_ref[...] = jnp.where(first, 0.0, yn).astype(BF16)

    @pl.when((j >= nsec) & (j < 2 * nsec))
    def _():
        yn = head_norm(gk_ref)
        kf_ref[...] = yn
        kb_ref[...] = yn.astype(BF16)

    @pl.when(j >= 2 * nsec)
    def _():
        vf_ref[...] = y
        if vt_rows:
            for r in range(y.shape[0] // vt_rows):
                outs[5][r] = y[r * vt_rows:(r + 1) * vt_rows, :].T.astype(BF16)


def _qkv(h, g, w, n, qk_gain, hd, layer=None, vt_rows=None, stack=None):
    m, d = h.shape
    emit = layer is not None
    tm = _pick(m, BIG_ROW_TILES)
    tn = _pick(n, (512, 256) if tm > 512 else (QKV_COL_TILE, 512, 256))
    cw = min(tn, 256)
    nsec = n // tn
    assert (not emit or m == tm) and (not vt_rows or tm % vt_rows == 0)

    def sec(j, s):
        return jnp.clip(j - s * nsec, 0, nsec - 1)

    lane = jnp.arange(cw)
    seg = (lane[:, None] // hd == lane[None, :] // hd).astype(BF16)
    gains = [jnp.tile(qk_gain[s].reshape(1, 2 * hd).astype(F32), (1, tn // (2 * hd))) for s in range(2)]
    in_specs = [
        pl.BlockSpec((tm, d), lambda i, j: (i, 0)),
        pl.BlockSpec((1, d), lambda i, j: (0, 0)),
        _weight_spec((d, tn), lambda i, j: (0, j), layer),
        pl.BlockSpec((1, tn), lambda i, j: (0, 0)),
        pl.BlockSpec((1, tn), lambda i, j: (0, 0)),
        pl.BlockSpec((cw, cw), lambda i, j: (0, 0)),
    ]
    args = [h, g.reshape(1, d), w, gains[0], gains[1], seg]

    def tile_spec(s):
        return pl.BlockSpec((tm, tn), lambda i, j: (i, sec(j, s)))

    aliases = {}
    if stack is not None:
        slot, slots, prev = stack

        def f32_spec(s):
            return pl.BlockSpec((None, tm, tn), lambda i, j: (slot, i, sec(j, s)))

        f32_shape = jax.ShapeDtypeStruct((slots, m, n), F32)
        if prev is not None:
            aliases = {len(args): 2, len(args) + 1: 4}
            in_specs += [pl.BlockSpec(memory_space=pl.ANY)] * 2
            args += list(prev)
    else:
        f32_spec, f32_shape = tile_spec, jax.ShapeDtypeStruct((m, n), F32)
    out_specs = [tile_spec(0), tile_spec(0), f32_spec(1), tile_spec(1), f32_spec(2)]
    out_shape = [jax.ShapeDtypeStruct((m, n), BF16)] * 2 + [f32_shape, jax.ShapeDtypeStruct((m, n), BF16),
                                                            f32_shape]
    if vt_rows:
        out_specs.append(pl.BlockSpec((tm // vt_rows, tn, vt_rows), lambda i, j: (i, sec(j, 2), 0)))
        out_shape.append(jax.ShapeDtypeStruct((m // vt_rows, n, vt_rows), BF16))
    if emit:
        out_specs.append(pl.BlockSpec((d, tn), lambda i, j: (0, j)))
        out_shape.append(jax.ShapeDtypeStruct((d, 3 * n), BF16))
    return pl.pallas_call(
        functools.partial(_qkv_kernel, hd=hd, nsec=nsec, vt_rows=vt_rows, n_prev=len(aliases), emit=emit),
        grid=(m // tm, 3 * nsec),
        in_specs=in_specs,
        out_specs=out_specs,
        out_shape=out_shape,
        scratch_shapes=[pltpu.VMEM((tm, d), BF16)],
        input_output_aliases=aliases,
        compiler_params=_cparams("parallel", "arbitrary"),
        name="qkv_proj_cast" if emit else "qkv_proj",
    )(*args)


def _mm_res_kernel(h_ref, x_ref, w_ref, o_ref, *rest, emit):
    w = w_ref[...].astype(BF16)
    if emit:
        rest[0][...] = w
    o_ref[...] = h_ref[...] + _dot(x_ref[...].astype(BF16), w)


def _matmul_residual(h, x, w, layer=None):
    m, n = h.shape
    k = x.shape[1]
    emit = layer is not None
    tm = _pick(m, BIG_ROW_TILES)
    tn = _pick(n, (512, 256, 128))
    assert not emit or m == tm
    out_specs = [pl.BlockSpec((tm, tn), lambda i, j: (i, j))]
    out_shape = [jax.ShapeDtypeStruct((m, n), F32)]
    if emit:
        out_specs.append(pl.BlockSpec((k, tn), lambda i, j: (0, j)))
        out_shape.append(jax.ShapeDtypeStruct((k, n), BF16))
    out = pl.pallas_call(
        functools.partial(_mm_res_kernel, emit=emit),
        grid=(m // tm, n // tn),
        in_specs=[
            pl.BlockSpec((tm, tn), lambda i, j: (i, j)),
            pl.BlockSpec((tm, k), lambda i, j: (i, 0)),
            _weight_spec((k, tn), lambda i, j: (0, j), layer),
        ],
        out_specs=out_specs,
        out_shape=out_shape,
        compiler_params=_cparams("parallel", "arbitrary"),
        name="matmul_residual_cast" if emit else "matmul_residual",
    )(h, x, w)
    return out if emit else out[0]


def _ssd_kernel(*refs, lc, valid, has_init, hpg, pdim, gps):
    (z_ref, x_ref, b_ref, c_ref, dt_ref) = refs[:5]
    pos = 5
    if has_init:
        tx_ref, tb_ref, tc_ref, h0_ref = refs[pos:pos + 4]
        pos += 4
    (cwx_ref, cwb_ref, cwc_ref, cbx_ref, cbb_ref, cbc_ref,
     dtb_ref, alog_ref, dsk_ref, ng_ref, hx_ref) = refs[pos:pos + 11]
    pos += 11
    y_ref, hout_ref = refs[pos:pos + 2]
    ht_ref, ex_ref, eb_ref, ec_ref = refs[pos + 2:]

    c = pl.program_id(2)
    w = hpg * pdim
    nst = ht_ref.shape[0]
    t0 = CONV_TAIL

    @pl.when(c == 0)
    def _():
        if has_init:
            ex_ref[0:t0, :] = tx_ref[0]
            eb_ref[0:t0, :] = tb_ref[0]
            ec_ref[0:t0, :] = tc_ref[0]
            ht_ref[...] = h0_ref[0].reshape(gps * w, nst).T
        else:
            ex_ref[0:t0, :] = jnp.zeros((t0, gps * w), F32)
            eb_ref[0:t0, :] = jnp.zeros((t0, gps * nst), F32)
            ec_ref[0:t0, :] = jnp.zeros((t0, gps * nst), F32)
            ht_ref[...] = jnp.zeros_like(ht_ref)

    def conv_act(e_ref, raw_ref, cw_ref, cb_ref, lanes):
        e_ref[t0:t0 + lc, lanes] = raw_ref[:, lanes]
        kw = cw_ref.shape[0]
        acc = cb_ref[:, lanes] + cw_ref[kw - 1:kw, lanes] * e_ref[t0:t0 + lc, lanes]
        for k in range(kw - 1):
            off = t0 - (kw - 1) + k
            acc = acc + cw_ref[k:k + 1, lanes] * e_ref[off:off + lc, lanes]
        e_ref[0:t0, lanes] = e_ref[lc:lc + t0, lanes]
        return _silu(acc)

    row = lax.broadcasted_iota(jnp.int32, (lc, lc), 0)
    col = lax.broadcasted_iota(jnp.int32, (lc, lc), 1)
    causal = row >= col
    tri = causal.astype(BF16)
    lane = lax.broadcasted_iota(jnp.int32, (lc, LANES), 1)
    first_half = lane < pdim

    def one_group(gi):
        wl = slice(gi * w, (gi + 1) * w)
        nl = slice(gi * nst, (gi + 1) * nst)
        hl = slice(gi * LANES, (gi + 1) * LANES)
        xc = conv_act(ex_ref, x_ref, cwx_ref, cbx_ref, wl)
        bc = conv_act(eb_ref, b_ref, cwb_ref, cbb_ref, nl)
        cc = conv_act(ec_ref, c_ref, cwc_ref, cbc_ref, nl)

        dtr = dt_ref[:, hl] + dtb_ref[:, hl]
        dt = jnp.maximum(dtr, 0.0) + jnp.log1p(jnp.exp(-jnp.abs(dtr)))
        if valid < lc:
            dt = jnp.where(lax.broadcasted_iota(jnp.int32, (lc, LANES), 0) < valid, dt, 0.0)
        da = dt * (-jnp.exp(alog_ref[:, hl]))
        d_hi, d_mid, d_lo = _split3(da)
        cs = _dot(tri, d_hi) + _dot(tri, d_mid) + _dot(tri, d_lo)
        cs_t = cs.T
        ecs = jnp.exp(cs)
        coef_s = dt * jnp.exp(cs[lc - 1:lc, :] - cs)

        e_hi, e_mid, e_lo = _split3(ecs)
        ex = _dot(jnp.concatenate([dt.astype(BF16), coef_s.astype(BF16), e_hi, e_mid, e_lo], axis=0),
                  hx_ref[...])
        dt_x, coef_x = ex[0:lc], ex[lc:2 * lc]
        ecs_x = ex[2 * lc:3 * lc] + ex[3 * lc:4 * lc] + ex[4 * lc:5 * lc]

        bcb = bc.astype(BF16)
        ccb = cc.astype(BF16)
        cb = _dot_nt(ccb, bcb)
        xd = xc * dt_x

        def decay_mat(j):
            col_j = jnp.sum(jnp.where(lane == j, cs, 0.0), axis=-1, keepdims=True)
            return (cb * jnp.exp(jnp.where(causal, col_j - cs_t[j:j + 1, :], -jnp.inf))).astype(BF16)

        y_blks = []
        for q in range(hpg // 2):
            mcat = jnp.concatenate([decay_mat(2 * q), decay_mat(2 * q + 1)], axis=1)
            xp = xd[:, q * LANES:(q + 1) * LANES]
            rhs = jnp.concatenate([jnp.where(first_half, xp, 0.0), jnp.where(first_half, 0.0, xp)],
                                  axis=0).astype(BF16)
            y_blks.append(_dot(mcat, rhs))
        y = y_blks[0] if len(y_blks) == 1 else jnp.concatenate(y_blks, axis=1)

        ht_in = ht_ref[:, wl]
        y = y + _dot(ccb, ht_in.astype(BF16)) * ecs_x
        upd = _dot(bc.T.astype(BF16), (xc * coef_x).astype(BF16))
        ht_ref[:, wl] = ht_in * ecs_x[lc - 1:lc, :] + upd

        y = y + xc * dsk_ref[:, wl]
        y = y * _silu(z_ref[:, wl])
        y = y * lax.rsqrt(jnp.mean(y * y, axis=-1, keepdims=True) + NORM_EPS) * ng_ref[:, wl]
        y_ref[:, wl] = y.astype(BF16)

    for gi in range(gps):
        one_group(gi)

    @pl.when(c == pl.num_programs(2) - 1)
    def _():
        hout_ref[0] = ht_ref[...].T.reshape(gps * hpg, pdim, nst)


def _ssd_core(zx, dtg, bsz, n_chunks, valid, dims, prm, init=None):
    d_inner, nst, groups, heads = dims
    hpg = heads // groups
    pdim = d_inner // heads
    w = hpg * pdim
    lc = SSD_CHUNK
    assert pdim * 2 == LANES and hpg % 2 == 0 and nst == LANES
    gps = _pick(groups, (SSD_GROUPS_PER_STEP, 1))
    gw, gn, gl = gps * w, gps * nst, gps * LANES
    rows = bsz * n_chunks * lc
    xo, bo, co = d_inner // gw, 2 * d_inner // gn, (2 * d_inner + groups * nst) // gn
    cbo, cco = d_inner // gn, (d_inner + groups * nst) // gn
    head_expand = (jnp.arange(LANES)[:, None] == jnp.arange(w)[None, :] // pdim).astype(BF16)

    def rowblk(b, g, c):
        return b * n_chunks + c

    in_specs = [
        pl.BlockSpec((lc, gw), lambda b, g, c: (rowblk(b, g, c), g)),
        pl.BlockSpec((lc, gw), lambda b, g, c: (rowblk(b, g, c), xo + g)),
        pl.BlockSpec((lc, gn), lambda b, g, c: (rowblk(b, g, c), bo + g)),
        pl.BlockSpec((lc, gn), lambda b, g, c: (rowblk(b, g, c), co + g)),
        pl.BlockSpec((lc, gl), lambda b, g, c: (rowblk(b, g, c), g)),
    ]
    args = [zx, zx, zx, zx, dtg]
    if init is not None:
        tail, h0, h0_layer = init
        in_specs += [
            pl.BlockSpec((1, CONV_TAIL, gw), lambda b, g, c: (b, 0, g)),
            pl.BlockSpec((1, CONV_TAIL, gn), lambda b, g, c: (b, 0, cbo + g)),
            pl.BlockSpec((1, CONV_TAIL, gn), lambda b, g, c: (b, 0, cco + g)),
            pl.BlockSpec((None, 1, gps * hpg, pdim, nst), lambda b, g, c: (h0_layer, b, g, 0, 0)),
        ]
        args += [tail, tail, tail, h0]
    kw = prm["conv_w"].shape[0]
    in_specs += [
        pl.BlockSpec((kw, gw), lambda b, g, c: (0, g)),
        pl.BlockSpec((kw, gn), lambda b, g, c: (0, cbo + g)),
        pl.BlockSpec((kw, gn), lambda b, g, c: (0, cco + g)),
        pl.BlockSpec((1, gw), lambda b, g, c: (0, g)),
        pl.BlockSpec((1, gn), lambda b, g, c: (0, cbo + g)),
        pl.BlockSpec((1, gn), lambda b, g, c: (0, cco + g)),
        pl.BlockSpec((1, gl), lambda b, g, c: (0, g)),
        pl.BlockSpec((1, gl), lambda b, g, c: (0, g)),
        pl.BlockSpec((1, gw), lambda b, g, c: (0, g)),
        pl.BlockSpec((1, gw), lambda b, g, c: (0, g)),
        pl.BlockSpec((LANES, w), lambda b, g, c: (0, 0)),
    ]
    args += [prm["conv_w"], prm["conv_w"], prm["conv_w"], prm["conv_b"], prm["conv_b"], prm["conv_b"],
             prm["dt_bias_g"], prm["a_log_g"], prm["d_x"], prm["norm_g"], head_expand]
    y, h_last = pl.pallas_call(
        functools.partial(_ssd_kernel, lc=lc, valid=valid, has_init=init is not None, hpg=hpg, pdim=pdim,
                          gps=gps),
        grid=(bsz, groups // gps, n_chunks),
        in_specs=in_specs,
        out_specs=[
            pl.BlockSpec((lc, gw), lambda b, g, c: (rowblk(b, g, c), g)),
            pl.BlockSpec((1, gps * hpg, pdim, nst), lambda b, g, c: (b, g, 0, 0)),
        ],
        out_shape=[jax.ShapeDtypeStruct((rows, d_inner), BF16),
                   jax.ShapeDtypeStruct((bsz, heads, pdim, nst), F32)],
        scratch_shapes=[pltpu.VMEM((nst, gw), F32),
                        pltpu.VMEM((lc + CONV_TAIL, gw), F32),
                        pltpu.VMEM((lc + CONV_TAIL, gn), F32),
                        pltpu.VMEM((lc + CONV_TAIL, gn), F32)],
        compiler_params=_cparams("arbitrary", "arbitrary", "arbitrary"),
        name="ssd_scan",
    )(*args)
    return y, h_last


def _attn_kernel(q1_ref, q2_ref, k_ref, vt_ref, slope_ref, lam_ref, sn_ref, o_ref, m_ref, a_ref,
                 *, tq, n_q, hw, lambda_init):
    qi = pl.program_id(2)
    hps = k_ref.shape[1] // hw
    key_row = lax.broadcasted_iota(jnp.int32, (tq, LANES), 0).astype(F32)
    m_ref[...] = jnp.full_like(m_ref, -jnp.inf)
    a_ref[...] = jnp.zeros_like(a_ref)
    ones_rows = jnp.ones((a_ref.shape[1] - hw, tq), BF16)

    def tile(e, ki, masked):
        lanes = slice(e * hw, (e + 1) * hw)
        slope2 = slope_ref[e][:, 0:1] * LOG2E
        k = k_ref[ki * tq:(ki + 1) * tq, lanes]
        vt = jnp.concatenate([vt_ref[ki, lanes, :], ones_rows], axis=0)
        kb = slope2 * (key_row + float(ki * tq))
        kb = jnp.concatenate([kb] * (tq // LANES), axis=1)
        scores = [_dot_nt(k, q_ref[:, lanes]) + kb for q_ref in (q1_ref, q2_ref)]
        if masked:
            keep = (lax.broadcasted_iota(jnp.int32, (tq, tq), 0)
                    <= lax.broadcasted_iota(jnp.int32, (tq, tq), 1))
            scores = [jnp.where(keep, s, -jnp.inf) for s in scores]
        slots = (2 * e, 2 * e + 1)
        m_olds = [m_ref[t] for t in slots]
        m_news = [jnp.maximum(m_old, _rows_reduce(s, jnp.max)) for m_old, s in zip(m_olds, scores)]
        probs = [jnp.exp2((s - m_new).astype(BF16)) for s, m_new in zip(scores, m_news)]
        pvs = [_dot(vt, p) for p in probs]
        for t, m_old, m_new, pv in zip(slots, m_olds, m_news, pvs):
            a_ref[t] = jnp.exp2(m_old - m_new) * a_ref[t] + pv
            m_ref[t] = m_new

    for v in range(n_q):
        @pl.when(qi == v)
        def _(v=v):
            for ki in range(v):
                for e in range(hps):
                    tile(e, ki, False)
            for e in range(hps):
                tile(e, v, True)

    lam = _lambda_full(lam_ref, lambda_init)
    for e in range(hps):
        a1, a2 = a_ref[2 * e], a_ref[2 * e + 1]
        ot = a1[0:hw, :] / a1[hw:hw + 1, :] - lam * (a2[0:hw, :] / a2[hw:hw + 1, :])
        ot = ot * lax.rsqrt(jnp.mean(ot * ot, axis=0, keepdims=True) + NORM_EPS)
        o_ref[:, e * hw:(e + 1) * hw] = (ot.T * (sn_ref[...] * (1.0 - lambda_init))).astype(BF16)


def _alibi_slopes(n_heads):
    return jnp.exp2(-8.0 * jnp.arange(1, n_heads + 1, dtype=F32) / n_heads)


def _attn_prompt(q1, q2, kb, vt, bsz, seq, n_heads, lam_p, sub_norm, lambda_init):
    m, width = q1.shape
    hw = width // n_heads
    tq = vt.shape[2]
    nq = seq // tq
    assert seq % tq == 0 and tq % LANES == 0
    hps = _pick(n_heads, (ATTN_HEADS_PER_STEP, 1))
    gw = hps * hw
    slopes = jnp.broadcast_to(_alibi_slopes(n_heads)[:, None, None], (n_heads, 1, LANES))
    return pl.pallas_call(
        functools.partial(_attn_kernel, tq=tq, n_q=nq, hw=hw, lambda_init=lambda_init),
        grid=(bsz, n_heads // hps, nq),
        in_specs=[
            pl.BlockSpec((tq, gw), lambda b, h, i: (b * nq + i, h)),
            pl.BlockSpec((tq, gw), lambda b, h, i: (b * nq + i, h)),
            pl.BlockSpec((seq, gw), lambda b, h, i: (b, h)),
            pl.BlockSpec((nq, gw, tq), lambda b, h, i: (b, h, 0)),
            pl.BlockSpec((hps, 1, LANES), lambda b, h, i: (h, 0, 0)),
            pl.BlockSpec(lam_p.shape, lambda b, h, i: (0, 0)),
            pl.BlockSpec((1, hw), lambda b, h, i: (0, 0)),
        ],
        out_specs=pl.BlockSpec((tq, gw), lambda b, h, i: (b * nq + i, h)),
        out_shape=jax.ShapeDtypeStruct((m, width), BF16),
        scratch_shapes=[pltpu.VMEM((2 * hps, 1, tq), F32), pltpu.VMEM((2 * hps, hw + BF16_ROWS, tq), F32)],
        compiler_params=_cparams("parallel", "parallel", "arbitrary"),
        name="attn_prompt",
    )(q1, q2, kb, vt, slopes, lam_p.astype(F32), sub_norm.reshape(1, hw).astype(F32))


def _attn_dec_kernel(*refs, pps, n_steps, n_tok, n_heads, lambda_init):
    q_ref = refs[1]
    kp_refs, vp_refs = refs[2:2 + pps], refs[2 + pps:2 + 2 * pps]
    (kn_ref, vn_ref, base_ref, basen_ref, slc_ref, lam_ref, sn_ref,
     o_ref, m_ref, l_ref, acc_ref) = refs[2 + 2 * pps:]
    st = pl.program_id(1)
    rph = 2 * n_tok

    @pl.when(st == 0)
    def _():
        m_ref[...] = jnp.full_like(m_ref, -jnp.inf)
        l_ref[...] = jnp.zeros_like(l_ref)
        acc_ref[...] = jnp.zeros_like(acc_ref)

    q = q_ref[0]

    def flat(x3):
        return x3.reshape(x3.shape[0] * x3.shape[1], x3.shape[2]).astype(BF16)

    def update(pages, bias_ref, shifts):
        m_old = m_ref[...]
        m_new = m_old
        scores = []
        for (k_ref3, _), shift in zip(pages, shifts):
            s = _dot_nt(q, flat(k_ref3())) + bias_ref[...]
            scores.append(s)
            m_new = jnp.maximum(m_new, jnp.max(s, axis=-1, keepdims=True) + shift)
        alpha = jnp.exp2(m_old - m_new)
        l_new = alpha * l_ref[...]
        acc = alpha * acc_ref[...]
        for s, (_, v_ref3), shift in zip(scores, pages, shifts):
            p = jnp.exp2(s - (m_new - shift))
            l_new = l_new + jnp.sum(p, axis=-1, keepdims=True)
            acc = acc + _dot(p.astype(BF16), flat(v_ref3()))
        l_ref[...] = l_new
        acc_ref[...] = acc
        m_ref[...] = m_new

    @pl.when(st < n_steps)
    def _():
        first = (st * pps).astype(F32)
        update([(lambda r=kp_refs[i]: r[0, 0], lambda r=vp_refs[i]: r[0, 0]) for i in range(pps)],
               base_ref, [slc_ref[...] * (first + float(i)) for i in range(pps)])

    @pl.when(st == n_steps)
    def _():
        update([(lambda: kn_ref[0], lambda: vn_ref[0])], basen_ref, [0.0])
        lam = _lambda_full(lam_ref, lambda_init)
        on = acc_ref[...] / l_ref[...]
        hw = on.shape[1]
        for h in range(n_heads):
            o = on[h * rph:h * rph + n_tok, :] - lam * on[h * rph + n_tok:(h + 1) * rph, :]
            o = o * lax.rsqrt(jnp.mean(o * o, axis=-1, keepdims=True) + NORM_EPS)
            o_ref[0, :, h * hw:(h + 1) * hw] = (o * (sn_ref[...] * (1.0 - lambda_init))).astype(o_ref.dtype)


def _attn_sample(q1, q2, k_new, v_new, cache_k, cache_v, layer, page_table, bsz, n_tok, n_heads,
                 lam_p, sub_norm, lambda_init):
    page, hw = cache_k.shape[2], cache_k.shape[4]
    n_pages = page_table.shape[1]
    past = n_pages * page
    rph = 2 * n_tok
    nrow = rph * n_heads
    ncol = page * n_heads
    assert n_tok <= page and rph % 8 == 0 and n_heads % 8 == 0
    q12 = jnp.stack([q1, q2]).reshape(2, bsz, n_tok, n_heads, hw)
    qs = jnp.transpose(q12, (1, 3, 0, 2, 4)).reshape(bsz, nrow, hw)
    pad = ((0, 0), (0, page - n_tok), (0, 0), (0, 0))
    kn = jnp.pad(k_new.reshape(bsz, n_tok, n_heads, hw), pad)
    vn = jnp.pad(v_new.reshape(bsz, n_tok, n_heads, hw), pad)
    r = jnp.arange(nrow)
    c = jnp.arange(ncol)
    row_head, row_tok = r // rph, (r % n_tok).astype(F32)
    col_key, col_head = (c // n_heads).astype(F32), c % n_heads
    slope2 = (_alibi_slopes(n_heads) * LOG2E)[row_head]
    same_head = row_head[:, None] == col_head[None, :]
    dist_page = past + row_tok[:, None] - col_key[None, :]
    base = jnp.where(same_head, -slope2[:, None] * dist_page, -jnp.inf)
    dist_new = row_tok[:, None] - col_key[None, :]
    ok_new = same_head & (col_key[None, :] < n_tok) & (dist_new >= 0)
    base_new = jnp.where(ok_new, -slope2[:, None] * dist_new, -jnp.inf)
    slope_page = (slope2 * page).reshape(nrow, 1)

    pps = _pick(n_pages, (DEC_PAGES_PER_STEP, 2, 1))
    n_steps = n_pages // pps

    def page_idx(i, b, p, pt):
        return (layer, pt[b, jnp.minimum(p, n_steps - 1) * pps + i], 0, 0, 0)

    def const2(b, p, pt):
        return (0, 0)

    page_specs = [pl.BlockSpec((1, 1, page, n_heads, hw), functools.partial(page_idx, i)) for i in range(pps)]
    grid_spec = pltpu.PrefetchScalarGridSpec(
        num_scalar_prefetch=1,
        grid=(bsz, n_steps + 1),
        in_specs=[
            pl.BlockSpec((1, nrow, hw), lambda b, p, pt: (b, 0, 0)),
            *page_specs,
            *page_specs,
            pl.BlockSpec((1, page, n_heads, hw), lambda b, p, pt: (b, 0, 0, 0)),
            pl.BlockSpec((1, page, n_heads, hw), lambda b, p, pt: (b, 0, 0, 0)),
            pl.BlockSpec((nrow, ncol), const2),
            pl.BlockSpec((nrow, ncol), const2),
            pl.BlockSpec((nrow, 1), const2),
            pl.BlockSpec(lam_p.shape, const2),
            pl.BlockSpec((1, hw), const2),
        ],
        out_specs=pl.BlockSpec((1, n_tok, n_heads * hw), lambda b, p, pt: (b, 0, 0)),
        scratch_shapes=[pltpu.VMEM((nrow, 1), F32), pltpu.VMEM((nrow, 1), F32), pltpu.VMEM((nrow, hw), F32)],
    )
    out = pl.pallas_call(
        functools.partial(_attn_dec_kernel, pps=pps, n_steps=n_steps, n_tok=n_tok, n_heads=n_heads,
                          lambda_init=lambda_init),
        grid_spec=grid_spec,
        out_shape=jax.ShapeDtypeStruct((bsz, n_tok, n_heads * hw), BF16),
        compiler_params=_cparams("arbitrary", "arbitrary"),
        name="attn_sample",
    )(page_table, qs, *([cache_k] * pps), *([cache_v] * pps), kn, vn, base, base_new, slope_page,
      lam_p.astype(F32), sub_norm.reshape(1, hw).astype(F32))
    return out.reshape(bsz * n_tok, n_heads * hw)


def _ssd_params(j, dims, ssd_in, conv_w, conv_b, dt_bias, a_log, d_skip, norm_g):
    d_inner, nst, groups, heads = dims
    hpg = heads // groups
    pdim = d_inner // heads
    conv_dim = conv_w.shape[2]
    d_model = ssd_in.shape[1]

    def per_group(v):
        return jnp.pad(v.astype(F32).reshape(groups, hpg), ((0, 0), (0, LANES - hpg))).reshape(1, groups * LANES)

    w_dt = lax.slice(ssd_in, (j, 0, d_inner + conv_dim), (j + 1, d_model, ssd_in.shape[2]))
    w_dt = w_dt.reshape(d_model, groups, hpg)
    w_dt = jnp.pad(w_dt, ((0, 0), (0, 0), (0, LANES - hpg))).reshape(d_model, groups * LANES)
    return dict(
        w_dt=w_dt.astype(BF16),
        conv_w=conv_w[j].astype(F32),
        conv_b=conv_b[j].reshape(1, conv_dim).astype(F32),
        dt_bias_g=per_group(dt_bias[j]),
        a_log_g=per_group(a_log[j]),
        d_x=jnp.repeat(d_skip[j].astype(F32), pdim).reshape(1, d_inner),
        norm_g=norm_g[j].reshape(1, d_inner).astype(F32),
    )


def kernel(x_prompt, x_sample, cache_k, cache_v, state_conv, state_ssm, page_table, p_prompt, p_sample, norm_ffa, ffa_in, ffa_out, norm_mix, norm_ffb, ffb_in, ffb_out, norm_ple, ple_proj, ple_gate, ssd_in, ssd_conv_w, ssd_conv_b, ssd_dt_bias, ssd_a_log, ssd_d, ssd_norm, ssd_out, attn_qkv, attn_qk_norm, attn_lambda, attn_sub_norm, attn_out):
    bp, seq, d_model = x_prompt.shape
    bs, n_tok, _ = x_sample.shape
    depth = norm_ffa.shape[0]
    n_mixers = 2
    n_attn = depth // n_mixers
    heads = ssd_dt_bias.shape[1]
    d_inner = ssd_out.shape[1]
    nst = state_ssm.shape[-1]
    conv_dim = ssd_conv_w.shape[2]
    kw = ssd_conv_w.shape[1]
    groups = (conv_dim - d_inner) // (2 * nst)
    dims = (d_inner, nst, groups, heads)
    n_heads = cache_k.shape[3]
    hw = cache_k.shape[4]
    hd = hw // 2
    att_w = n_heads * hw
    lc = SSD_CHUNK
    assert seq % lc == 0 and n_tok <= lc and n_tok >= kw - 1 and kw - 1 <= CONV_TAIL

    n_zx = d_inner + conv_dim
    ple_proj_b, ple_gate_b = ple_proj.astype(BF16), ple_gate.astype(BF16)
    ssd_in_b = ssd_in.astype(BF16)
    state_ssm = state_ssm.astype(F32)

    def ffn_both(hs, g, w_in, w_out, layer):
        h_s, wa, wb, wo = _ffn(hs[1], g, w_in, w_in, w_out, layer=layer)
        return [_ffn(hs[0], g, wa, wb, wo), h_s]

    h = [x_prompt.reshape(bp * seq, d_model), x_sample.reshape(bs * n_tok, d_model)]
    ple_in = [p_prompt.reshape(depth, bp * seq, -1), p_sample.reshape(depth, bs * n_tok, -1)]
    kp_stack = vp_stack = kv_prev = None
    ks_l, vs_l = [], []
    conv_out, ssm_out = [[], []], [[], []]

    for i in range(depth):
        j = i // n_mixers
        h = ffn_both(h, norm_ffa[i], ffa_in, ffa_out, i)
        if i % n_mixers == 0:
            prm = _ssd_params(j, dims, ssd_in_b, ssd_conv_w, ssd_conv_b, ssd_dt_bias, ssd_a_log, ssd_d, ssd_norm)
            zs, dts = _ssd_inproj(h[1], norm_mix[i], ssd_in_b, j, n_zx, prm["w_dt"])
            zs3 = zs.reshape(bs, n_tok, -1)
            pad_rows = ((0, 0), (0, lc - n_tok), (0, 0))
            zs_pad = jnp.pad(zs3, pad_rows).reshape(bs * lc, -1)
            dts_pad = jnp.pad(dts.reshape(bs, n_tok, -1), pad_rows).reshape(bs * lc, -1)
            tail = jnp.pad(state_conv[j].astype(F32), ((0, 0), (CONV_TAIL - (kw - 1), 0), (0, 0)))
            y, h_last = _ssd_core(zs_pad, dts_pad, bs, 1, n_tok, dims, prm, init=(tail, state_ssm, j))
            xbc = zs3[:, :, d_inner:]
            conv_out[1].append(jnp.concatenate([state_conv[j].astype(F32), xbc], axis=1)[:, n_tok:])
            ssm_out[1].append(h_last)
            y = y.reshape(bs, lc, d_inner)[:, :n_tok].reshape(bs * n_tok, d_inner)
            h[1], w_out_b = _matmul_residual(h[1], y, ssd_out, layer=j)
            zx, dtg = _ssd_inproj(h[0], norm_mix[i], ssd_in_b, j, n_zx, prm["w_dt"])
            y, h_last = _ssd_core(zx, dtg, bp, seq // lc, lc, dims, prm)
            conv_out[0].append(zx.reshape(bp, seq, -1)[:, seq - (kw - 1):, d_inner:])
            ssm_out[0].append(h_last)
            h[0] = _matmul_residual(h[0], y, w_out_b)
        else:
            lambda_init = 0.8 - 0.6 * math.exp(-0.3 * i)
            qk_gain = attn_qk_norm[j]
            q1, q2, kf, _, vf, w_qkv_b = _qkv(h[1], norm_mix[i], attn_qkv, att_w, qk_gain, hd, layer=j)
            o = _attn_sample(q1, q2, kf, vf, cache_k, cache_v, j, page_table, bs, n_tok, n_heads,
                             attn_lambda[j], attn_sub_norm[j], lambda_init)
            ks_l.append(kf.reshape(bs, n_tok, n_heads, hw))
            vs_l.append(vf.reshape(bs, n_tok, n_heads, hw))
            h[1], w_out_b = _matmul_residual(h[1], o, attn_out, layer=j)
            q1, q2, kp_stack, kb, vp_stack, vt = _qkv(
                h[0], norm_mix[i], w_qkv_b, att_w, qk_gain, hd, vt_rows=_pick(seq, (512, 256, 128)),
                stack=(j, n_attn, kv_prev))
            kv_prev = (kp_stack, vp_stack)
            o = _attn_prompt(q1, q2, kb, vt, bp, seq, n_heads, attn_lambda[j], attn_sub_norm[j], lambda_init)
            h[0] = _matmul_residual(h[0], o, w_out_b)
        h = ffn_both(h, norm_ffb[i], ffb_in, ffb_out, i)
        h = [_ple(x, ple_in[gi][i], norm_ple[i], ple_proj_b, ple_gate_b, i) for gi, x in enumerate(h)]

    return (h[0].reshape(bp, seq, d_model), h[1].reshape(bs, n_tok, d_model),
            kp_stack.reshape(n_attn, bp, seq, n_heads, hw), vp_stack.reshape(n_attn, bp, seq, n_heads, hw),
            jnp.stack(ks_l), jnp.stack(vs_l),
            jnp.stack(conv_out[0]), jnp.stack(ssm_out[0]), jnp.stack(conv_out[1]), jnp.stack(ssm_out[1]))
```
